```python
import jax, jax.numpy as jnp
from jax import lax
import numpy as np

D_MODEL = 1024
BATCH = 2
SEQ = 8192
DEPTH = 4

CHUNK = 64
GM_BLOCK = 128
GM_HEADS = 4
GM_HEAD_DIM = 128
GM_DIM = GM_HEADS * GM_HEAD_DIM
HG_HEADS = 4
HG_DK = 128
HG_DV = 128
HG_FDIM = HG_HEADS * HG_DK
HG_IDIM = HG_HEADS * HG_DV
N_BRANCH = 2
D_FF = 2816
CONV_W = 3
EPS = 1e-6
TINY = 1e-30
LB_MAX = 0.999
IN_SIZES = (GM_DIM, GM_DIM, HG_FDIM, HG_FDIM, HG_IDIM, HG_IDIM, N_BRANCH * D_MODEL)
IN_COLS = 2 * GM_DIM + 2 * HG_FDIM + 2 * HG_IDIM + N_BRANCH * D_MODEL

kernel_name = "hybrid_gmlp_hgrn2_convffn_trunk"


def _split_cols(z, sizes):
    outs, start = [], 0
    for s in sizes:
        outs.append(z[..., start:start + s])
        start += s
    return outs


def _rmsnorm(x, g):
    xf = x.astype(jnp.float32)
    r = lax.rsqrt(jnp.mean(xf * xf, axis=-1, keepdims=True) + EPS)
    return (xf * r).astype(x.dtype) * g


def _layernorm(x, g, b):
    xf = x.astype(jnp.float32)
    mu = jnp.mean(xf, axis=-1, keepdims=True)
    var = jnp.mean(jnp.square(xf - mu), axis=-1, keepdims=True)
    return ((xf - mu) * lax.rsqrt(var + EPS)).astype(x.dtype) * g + b


def _gmlp_branch(u, v, ln_g, ln_b, ws, bs):
    B, S, _ = v.shape
    nb = S // GM_BLOCK
    v = _layernorm(v, ln_g, ln_b)
    vb = v.reshape(B, nb, GM_BLOCK, GM_HEADS, GM_HEAD_DIM)
    pos = jnp.arange(GM_BLOCK) // CHUNK
    mask = pos[:, None] >= pos[None, :]
    w = jnp.where(mask[None], ws, 0.0).astype(v.dtype)
    mixed = jnp.einsum('hij,bnjhd->bnihd', w, vb) + bs.T[None, None, :, :, None]
    return u * mixed.reshape(B, S, GM_DIM)


def _hgrn2_chunk_step(state, inp):
    q, k, g, v = inp
    C = q.shape[2]
    G = jnp.cumsum(g, axis=2)
    causal = jnp.tril(jnp.ones((C, C), dtype=bool))[None, None, :, :, None]
    diff = G[:, :, :, None, :] - G[:, :, None, :, :]
    decay = jnp.where(causal, jnp.exp(jnp.minimum(diff, 0.0)), 0.0)
    A = jnp.einsum('bhid,bhijd,bhjd->bhij', q, decay, k)
    o = jnp.einsum('bhij,bhje->bhie', A, v) + jnp.einsum('bhid,bhde->bhie', q * jnp.exp(G), state)
    G_last = G[:, :, -1:, :]
    state = jnp.exp(G_last[:, :, 0, :])[..., None] * state + \
        jnp.einsum('bhjd,bhje->bhde', k * jnp.exp(G_last - G), v)
    return state, o


def _hgrn2_branch(q, f, i, og, lb, norm_g):
    B, S, _ = q.shape
    nc = S // CHUNK
    out_dtype = i.dtype
    qf = jax.nn.silu(q.astype(jnp.float32))
    ff = f.astype(jnp.float32)
    lb = jnp.clip(lb.astype(jnp.float32), 0.0, LB_MAX)
    forget = lb + (1.0 - lb) * jax.nn.sigmoid(ff)
    g_log = jnp.log(jnp.maximum(forget, TINY))
    k = (1.0 - lb) * jax.nn.sigmoid(-ff)
    vf = i.astype(jnp.float32)

    def to_chunks(t, d):
        return t.reshape(B, nc, CHUNK, HG_HEADS, d).transpose(1, 0, 3, 2, 4)

    xs = (to_chunks(qf, HG_DK), to_chunks(k, HG_DK), to_chunks(g_log, HG_DK), to_chunks(vf, HG_DV))
    s0 = jnp.zeros((B, HG_HEADS, HG_DK, HG_DV), jnp.float32)
    _, o = lax.scan(_hgrn2_chunk_step, s0, xs)
    o = o.transpose(1, 0, 3, 2, 4).reshape(B, S, HG_HEADS, HG_DV).astype(out_dtype)
    o = _rmsnorm(o, norm_g) * jax.nn.silu(og.reshape(B, S, HG_HEADS, HG_DV))
    return o.reshape(B, S, HG_IDIM)


def _conv_ffn(h, w_up, conv_w, conv_b, w_down):
    S = h.shape[1]
    z = h @ w_up
    zp = jnp.pad(z, ((0, 0), (CONV_W - 1, 0), (0, 0)))
    zc = conv_b + sum(zp[:, t:t + S, :] * conv_w[t] for t in range(CONV_W))
    gate, val = zc[..., :D_FF], zc[..., D_FF:]
    return (jax.nn.silu(gate) * val) @ w_down


def setup_inputs(seed: int = 0) -> dict:
    key = jax.random.key(seed)
    ks = jax.random.split(key, 20)
    f32 = jnp.float32

    def nrm(k, shape, scale):
        return jax.random.normal(k, shape, f32) * scale

    return {
        "x": nrm(ks[0], (BATCH, SEQ, D_MODEL), 1.0),
        "mix_norm": 1.0 + nrm(ks[1], (DEPTH, D_MODEL), 0.05),
        "w_in": nrm(ks[2], (DEPTH, D_MODEL, IN_COLS), D_MODEL ** -0.5),
        "gm_ln_g": 1.0 + nrm(ks[3], (DEPTH, GM_DIM), 0.05),
        "gm_ln_b": nrm(ks[4], (DEPTH, GM_DIM), 0.02),
        "gm_ws": nrm(ks[5], (DEPTH, GM_HEADS, GM_BLOCK, GM_BLOCK), GM_BLOCK ** -0.5),
        "gm_bs": 1.0 + nrm(ks[6], (DEPTH, GM_HEADS, GM_BLOCK), 0.1),
        "hg_lb_logits": nrm(ks[7], (DEPTH, HG_FDIM), 0.5),
        "hg_norm_g": 1.0 + nrm(ks[8], (DEPTH, HG_DV), 0.05),
        "w_br_gm": nrm(ks[9], (DEPTH, GM_DIM, D_MODEL), GM_DIM ** -0.5),
        "w_br_hg": nrm(ks[10], (DEPTH, HG_IDIM, D_MODEL), HG_IDIM ** -0.5),
        "w_out": nrm(ks[11], (DEPTH, D_MODEL, D_MODEL), D_MODEL ** -0.5),
        "ffn_norm": 1.0 + nrm(ks[12], (DEPTH, D_MODEL), 0.05),
        "w_up": nrm(ks[13], (DEPTH, D_MODEL, 2 * D_FF), D_MODEL ** -0.5),
        "conv_w": nrm(ks[14], (DEPTH, CONV_W, 2 * D_FF), CONV_W ** -0.5),
        "conv_b": nrm(ks[15], (DEPTH, 2 * D_FF), 0.02),
        "w_down": nrm(ks[16], (DEPTH, D_FF, D_MODEL), D_FF ** -0.5),
        "final_norm": 1.0 + nrm(ks[17], (D_MODEL,), 0.05),
    }


def reference(x, mix_norm, w_in, gm_ln_g, gm_ln_b, gm_ws, gm_bs, hg_lb_logits, hg_norm_g,
              w_br_gm, w_br_hg, w_out, ffn_norm, w_up, conv_w, conv_b, w_down, final_norm):
    p = jax.nn.softmax(hg_lb_logits.astype(jnp.float32), axis=0)
    lower_bounds = jnp.cumsum(p, axis=0) - p[0:1]
    for l in range(DEPTH):
        h = _rmsnorm(x, mix_norm[l])
        z = h @ w_in[l]
        u, v, q, f, i, og, gates = _split_cols(z, IN_SIZES)
        a = _gmlp_branch(jax.nn.gelu(u, approximate=False), jax.nn.gelu(v, approximate=False),
                         gm_ln_g[l], gm_ln_b[l], gm_ws[l], gm_bs[l])
        b = _hgrn2_branch(q, f, i, og, lower_bounds[l], hg_norm_g[l])
        gate_a, gate_b = gates[..., :D_MODEL], gates[..., D_MODEL:]
        y = jax.nn.sigmoid(gate_a) * (a @ w_br_gm[l]) + jax.nn.sigmoid(gate_b) * (b @ w_br_hg[l])
        x = x + y @ w_out[l]
        h = _rmsnorm(x, ffn_norm[l])
        x = x + _conv_ffn(h, w_up[l], conv_w[l], conv_b[l], w_down[l])
    return _rmsnorm(x, final_norm)
```

```python
import functools

import numpy as np
import jax
import jax.numpy as jnp
from jax import lax
from jax.experimental import pallas as pl
from jax.experimental.pallas import tpu as pltpu

D_MODEL = 1024
STREAM_CHUNK = 64
GM_BLOCK = 128
GM_HEADS = 4
GM_HEAD_DIM = 128
GM_DIM = GM_HEADS * GM_HEAD_DIM
HG_HEADS = 4
HG_DK = 128
HG_DV = 128
HG_DIM = HG_HEADS * HG_DK
D_FF = 2816
CONV_W = 3
EPS = 1e-6
TINY = 1e-30
LB_MAX = 0.999

COL_UV = 0
COL_HG = 2 * GM_DIM
COL_GATES = COL_HG + 4 * HG_DIM
IN_COLS = COL_GATES + 2 * D_MODEL

HG_CHUNK = 64
HG_LEVELS = 6
TM_MIX = 256
TM_FFN = 256
FFN_COL_BLOCK = 1408
CARRY_ROWS = 8
VMEM_LIMIT_BYTES = 56 * 1024 * 1024

_F32 = jnp.float32
_BF16 = jnp.bfloat16


def _range_sum_matrix(c):
    t = np.arange(c)
    mats = [t[None, :] <= t[:, None], t[None, :] > t[:, None]]
    s = c // 2
    while s >= 1:
        x = np.zeros((c, c), bool)
        for i in range(c):
            base = (i // (2 * s)) * 2 * s
            mid = base + s - 1
            if i - base >= s:
                x[i, mid + 1:i + 1] = True
            else:
                x[i, i + 1:mid + 1] = True
        mats.append(x)
        s //= 2
    p = np.concatenate(mats, 0).astype(np.float32)
    return np.concatenate([p, p, p], 1)


def _level_masks(c):
    i = np.arange(c)[:, None]
    j = np.arange(c)[None, :]
    out = []
    s = c // 2
    while s >= 1:
        out.append((i // (2 * s) == j // (2 * s)) & (i % (2 * s) >= s) & (j % (2 * s) < s))
        s //= 2
    out.append(i == j)
    return np.stack(out).astype(np.float32)


def _rms(x, gain):
    r = lax.rsqrt(jnp.mean(x * x, axis=-1, keepdims=True) + EPS)
    return (x * r) * gain


def _sigmoid(x):
    return 1.0 / (1.0 + jnp.exp(-x))


def _gelu_exact(x):
    return 0.5 * x * (1.0 + lax.erf(x * np.float32(np.sqrt(0.5))))


def _dot(a, b):
    return jnp.dot(a, b, preferred_element_type=_F32)


def _dot_nt(a, b):
    return lax.dot_general(a, b, (((1,), (1,)), ((), ())), preferred_element_type=_F32)


def _dot_tn(a, b):
    return lax.dot_general(a, b, (((0,), (0,)), ((), ())), preferred_element_type=_F32)


def _mixer_kernel(layer, tm,
                  x_ref, nrm_ref, win_ref, lng_ref, lnb_ref, ws_ref, bst_ref, lbl_ref, hgn_ref,
                  wgm_ref, whg_ref, wout_ref, p_ref, msk_ref,
                  o_ref,
                  st_ref, q_scr, k_scr, g_scr, v_scr, b_scr):
    @pl.when(pl.program_id(1) == 0)
    def _():
        st_ref[...] = jnp.zeros_like(st_ref)

    x = x_ref[...]
    h = _rms(x, nrm_ref[...]).astype(_BF16)

    uv = _dot(h, win_ref[:, COL_UV:COL_UV + 2 * GM_DIM])
    ug = _gelu_exact(uv[:, :GM_DIM])
    vg = _gelu_exact(uv[:, GM_DIM:])
    mu = jnp.mean(vg, axis=-1, keepdims=True)
    vc = vg - mu
    var = jnp.mean(vc * vc, axis=-1, keepdims=True)
    vln = ((vc * lax.rsqrt(var + EPS)) * lng_ref[...] + lnb_ref[...]).astype(_BF16)
    pi = lax.broadcasted_iota(jnp.int32, (GM_BLOCK, GM_BLOCK), 0) // STREAM_CHUNK
    pj = lax.broadcasted_iota(jnp.int32, (GM_BLOCK, GM_BLOCK), 1) // STREAM_CHUNK
    causal = pi >= pj
    wms = [jnp.where(causal, ws_ref[hh], 0.0).astype(_BF16) for hh in range(GM_HEADS)]
    blocks = []
    for n in range(tm // GM_BLOCK):
        heads = []
        for hh in range(GM_HEADS):
            vb = vln[n * GM_BLOCK:(n + 1) * GM_BLOCK, hh * GM_HEAD_DIM:(hh + 1) * GM_HEAD_DIM]
            heads.append(_dot(wms[hh], vb) + bst_ref[:, hh:hh + 1])
        blocks.append(jnp.concatenate(heads, axis=1))
    mixed = jnp.concatenate(blocks, axis=0)
    a = (ug * mixed).astype(_BF16)
    ya = _dot(a, wgm_ref[...])

    qfio = _dot(h, win_ref[:, COL_HG:COL_HG + 4 * HG_DIM])
    q = qfio[:, 0:HG_DIM]
    f = qfio[:, HG_DIM:2 * HG_DIM]
    logits = lbl_ref[...]
    e = jnp.exp(logits - jnp.max(logits, axis=0, keepdims=True))
    p = e / jnp.sum(e, axis=0, keepdims=True)
    lb = jnp.zeros((1, HG_DIM), _F32)
    for j in range(1, layer + 1):
        lb = lb + p[j:j + 1, :]
    lb = jnp.clip(lb, 0.0, LB_MAX)
    q_scr[...] = q * _sigmoid(q)
    forget = lb + (1.0 - lb) * _sigmoid(f)
    g_scr[...] = jnp.log(jnp.maximum(forget, TINY))
    k_scr[...] = (1.0 - lb) * _sigmoid(-f)
    v_scr[...] = qfio[:, 2 * HG_DIM:3 * HG_DIM]
    og = qfio[:, 3 * HG_DIM:4 * HG_DIM]
    ogs = og * _sigmoid(og)

    c = HG_CHUNK

    def chunk_body(ci, carry):
        r0 = pl.multiple_of(ci * c, c)
        rows = pl.ds(r0, c)
        g = g_scr[rows, :]
        g_hi = g.astype(_BF16)
        g_r1 = g - g_hi.astype(_F32)
        g_mid = g_r1.astype(_BF16)
        g_lo = (g_r1 - g_mid.astype(_F32)).astype(_BF16)
        sums = _dot(p_ref[...], jnp.concatenate([g_hi, g_mid, g_lo], axis=0))
        qt = q_scr[rows, :]
        kk = k_scr[rows, :]
        vv = v_scr[rows, :].astype(_BF16)
        for hh in range(HG_HEADS):
            hs = slice(hh * HG_DK, (hh + 1) * HG_DK)
            qh = qt[:, hs]
            kh = kk[:, hs]
            vh = vv[:, hs]
            gcum = sums[0:c, hs]
            st = st_ref[hh]
            o = _dot_nt((qh * jnp.exp(gcum)).astype(_BF16), st.astype(_BF16))
            amat = msk_ref[HG_LEVELS] * _dot_nt(qh.astype(_BF16), kh.astype(_BF16))
            for lv in range(HG_LEVELS):
                ex = jnp.exp(sums[(2 + lv) * c:(3 + lv) * c, hs])
                amat = amat + msk_ref[lv] * _dot_nt((qh * ex).astype(_BF16), (kh * ex).astype(_BF16))
            o = o + _dot(amat.astype(_BF16), vh)
            kd = (kh * jnp.exp(sums[c:2 * c, hs])).astype(_BF16)
            st_ref[hh] = st * jnp.exp(gcum[c - 1:c, :]) + _dot_tn(vh, kd)
            b_scr[rows, hs] = _rms(o, hgn_ref[...])
        return carry

    lax.fori_loop(0, tm // c, chunk_body, 0)

    b = (b_scr[...] * ogs).astype(_BF16)
    yb = _dot(b, whg_ref[...])

    gates = _dot(h, win_ref[:, COL_GATES:COL_GATES + 2 * D_MODEL])
    y = _sigmoid(gates[:, :D_MODEL]) * ya + _sigmoid(gates[:, D_MODEL:]) * yb
    o_ref[...] = x + _dot(y.astype(_BF16), wout_ref[...])


def _ffn_kernel(tm, final,
                x_ref, nrm_ref, wup_ref, cw_ref, cb_ref, wdn_ref, fin_ref,
                o_ref,
                carry_ref, zbuf_ref):
    @pl.when(pl.program_id(1) == 0)
    def _():
        carry_ref[...] = jnp.zeros_like(carry_ref)

    x = x_ref[...]
    h = _rms(x, nrm_ref[...]).astype(_BF16)
    w = FFN_COL_BLOCK

    def conv(col0):
        z = _dot(h, wup_ref[:, col0:col0 + w])
        zbuf_ref[0:CARRY_ROWS, :] = carry_ref[:, col0:col0 + w]
        zbuf_ref[CARRY_ROWS:CARRY_ROWS + tm, :] = z
        z1 = zbuf_ref[CARRY_ROWS - 1:CARRY_ROWS - 1 + tm, :]
        z2 = zbuf_ref[CARRY_ROWS - 2:CARRY_ROWS - 2 + tm, :]
        carry_ref[:, col0:col0 + w] = z[tm - CARRY_ROWS:tm, :]
        cw = cw_ref[:, col0:col0 + w]
        return cb_ref[:, col0:col0 + w] + cw[0:1, :] * z2 + cw[1:2, :] * z1 + cw[2:3, :] * z

    acc = x
    for j in range(D_FF // w):
        gate = conv(j * w)
        val = conv(D_FF + j * w)
        act = (gate * _sigmoid(gate) * val).astype(_BF16)
        acc = acc + _dot(act, wdn_ref[j * w:(j + 1) * w, :])
    if final:
        acc = _rms(acc, fin_ref[...])
    o_ref[...] = acc


def _const_spec(shape):
    nd = len(shape)
    return pl.BlockSpec(shape, lambda b, s: (0,) * nd)


def _mixer_call(layer, n_batch, n_seq, x2, nrm, win, lng, lnb, ws, bst, lbl, hgn, wgm, whg, wout, pmat, masks):
    tm = TM_MIX
    ns = n_seq // tm
    row_spec = pl.BlockSpec((tm, D_MODEL), lambda b, s: (b * ns + s, 0))
    consts = (nrm, win, lng, lnb, ws, bst, lbl, hgn, wgm, whg, wout, pmat, masks)
    return pl.pallas_call(
        functools.partial(_mixer_kernel, layer, tm),
        grid=(n_batch, ns),
        in_specs=[row_spec] + [_const_spec(c.shape) for c in consts],
        out_specs=row_spec,
        out_shape=jax.ShapeDtypeStruct(x2.shape, _F32),
        scratch_shapes=[
            pltpu.VMEM((HG_HEADS, HG_DV, HG_DK), _F32),
            pltpu.VMEM((tm, HG_DIM), _F32),
            pltpu.VMEM((tm, HG_DIM), _F32),
            pltpu.VMEM((tm, HG_DIM), _F32),
            pltpu.VMEM((tm, HG_DIM), _F32),
            pltpu.VMEM((tm, HG_DIM), _F32),
        ],
        compiler_params=pltpu.CompilerParams(
            dimension_semantics=("arbitrary", "arbitrary"),
            vmem_limit_bytes=VMEM_LIMIT_BYTES),
        name=f"mixer_l{layer}",
    )(x2, *consts)


def _ffn_call(layer, final, n_batch, n_seq, x2, nrm, wup, cw, cb, wdn, fin):
    tm = TM_FFN
    ns = n_seq // tm
    row_spec = pl.BlockSpec((tm, D_MODEL), lambda b, s: (b * ns + s, 0))
    consts = (nrm, wup, cw, cb, wdn, fin)
    return pl.pallas_call(
        functools.partial(_ffn_kernel, tm, final),
        grid=(n_batch, ns),
        in_specs=[row_spec] + [_const_spec(c.shape) for c in consts],
        out_specs=row_spec,
        out_shape=jax.ShapeDtypeStruct(x2.shape, _F32),
        scratch_shapes=[
            pltpu.VMEM((CARRY_ROWS, 2 * D_FF), _F32),
            pltpu.VMEM((CARRY_ROWS + tm, FFN_COL_BLOCK), _F32),
        ],
        compiler_params=pltpu.CompilerParams(
            dimension_semantics=("arbitrary", "arbitrary"),
            vmem_limit_bytes=VMEM_LIMIT_BYTES),
        name=f"ffn_l{layer}",
    )(x2, *consts)


def kernel(x, mix_norm, w_in, gm_ln_g, gm_ln_b, gm_ws, gm_bs, hg_lb_logits, hg_norm_g, w_br_gm, w_br_hg,
           w_out, ffn_norm, w_up, conv_w, conv_b, w_down, final_norm):
    n_batch, n_seq, d = x.shape
    depth = w_in.shape[0]
    assert d == D_MODEL and w_in.shape[2] == IN_COLS
    assert n_seq % TM_MIX == 0 and n_seq % TM_FFN == 0 and TM_MIX % GM_BLOCK == 0 and TM_MIX % HG_CHUNK == 0
    pmat = jnp.asarray(_range_sum_matrix(HG_CHUNK), _BF16)
    masks = jnp.asarray(_level_masks(HG_CHUNK), _F32)
    x2 = x.reshape(n_batch * n_seq, d)
    row = lambda v: v.reshape(1, -1)
    for l in range(depth):
        x2 = _mixer_call(
            l, n_batch, n_seq, x2, row(mix_norm[l]), w_in[l].astype(_BF16), row(gm_ln_g[l]), row(gm_ln_b[l]),
            gm_ws[l], gm_bs[l].T, hg_lb_logits, row(hg_norm_g[l]),
            w_br_gm[l].astype(_BF16), w_br_hg[l].astype(_BF16), w_out[l].astype(_BF16), pmat, masks)
        x2 = _ffn_call(
            l, l == depth - 1, n_batch, n_seq, x2, row(ffn_norm[l]), w_up[l].astype(_BF16), conv_w[l],
            row(conv_b[l]), w_down[l].astype(_BF16), row(final_norm))
    return x2.reshape(n_batch, n_seq, d)
```

```python
import functools

import numpy as np
import jax
import jax.numpy as jnp
from jax import lax
from jax.experimental import pallas as pl
from jax.experimental.pallas import tpu as pltpu

D_MODEL = 1024
STREAM_CHUNK = 64
GM_BLOCK = 128
GM_HEADS = 4
GM_HEAD_DIM = 128
GM_DIM = GM_HEADS * GM_HEAD_DIM
HG_HEADS = 4
HG_DK = 128
HG_DV = 128
HG_DIM = HG_HEADS * HG_DK
D_FF = 2816
CONV_W = 3
EPS = 1e-6
TINY = 1e-30
LB_MAX = 0.999

COL_UV = 0
COL_HG = 2 * GM_DIM
COL_GATES = COL_HG + 4 * HG_DIM
IN_COLS = COL_GATES + 2 * D_MODEL

HG_CHUNK = 64
HG_LEVELS = 6
TM_MIX = 512
TM_FFN = 512
MXU_DIM = 256
FFN_COL_BLOCKS = (1536, 1280)
CARRY_ROWS = 8
VMEM_LIMIT_BYTES = 56 * 1024 * 1024

_F32 = jnp.float32
_BF16 = jnp.bfloat16


def _range_sum_matrix(c):
    t = np.arange(c)
    mats = [t[None, :] <= t[:, None]]
    s = c // 2
    while s >= 1:
        x = np.zeros((c, c), bool)
        for i in range(c):
            base = (i // (2 * s)) * 2 * s
            mid = base + s - 1
            if i - base >= s:
                x[i, mid + 1:i + 1] = True
            else:
                x[i, i + 1:mid + 1] = True
        mats.append(x)
        s //= 2
    p = np.concatenate(mats, 0).astype(np.float32)
    return np.concatenate([p, p, p], 1)


def _level_masks(c):
    i = np.arange(c)[:, None]
    j = np.arange(c)[None, :]
    out = []
    s = c // 2
    while s >= 1:
        out.append((i // (2 * s) == j // (2 * s)) & (i % (2 * s) >= s) & (j % (2 * s) < s))
        s //= 2
    out.append(i == j)
    wide = np.zeros((len(out), c, 2 * c), np.float32)
    for n, m in enumerate(out):
        wide[n, :, (n % 2) * c:(n % 2 + 1) * c] = m
    return wide


def _rms(x, gain):
    r = lax.rsqrt(jnp.mean(x * x, axis=-1, keepdims=True) + EPS)
    return (x * r) * gain


def _sigmoid(x):
    return 1.0 / (1.0 + jnp.exp(-x))


def _gelu_exact(x):
    return 0.5 * x * (1.0 + lax.erf(x * np.float32(np.sqrt(0.5))))


def _dot(a, b):
    return jnp.dot(a, b, preferred_element_type=_F32)


def _dot_nt(a, b):
    return lax.dot_general(a, b, (((1,), (1,)), ((), ())), preferred_element_type=_F32)


def _dot_tn(a, b):
    return lax.dot_general(a, b, (((0,), (0,)), ((), ())), preferred_element_type=_F32)


def _mixer_kernel(layer, tm,
                  x_ref, nrm_ref, win_ref, lng_ref, lnb_ref, ws_ref, bst_ref, lbl_ref, hgn_ref,
                  wgm_ref, whg_ref, wout_ref, p_ref, msk_ref,
                  o_ref,
                  st_ref, q_scr, k_scr, g_scr, v_scr, b_scr, gc_scr, e_scr, kt_scr, stt_scr):
    @pl.when(pl.program_id(1) == 0)
    def _():
        st_ref[...] = jnp.zeros_like(st_ref)

    x = x_ref[...]
    h = _rms(x, nrm_ref[...]).astype(_BF16)

    uv = _dot(h, win_ref[:, COL_UV:COL_UV + 2 * GM_DIM])
    ug = _gelu_exact(uv[:, :GM_DIM])
    vg = _gelu_exact(uv[:, GM_DIM:])
    mu = jnp.mean(vg, axis=-1, keepdims=True)
    vc = vg - mu
    var = jnp.mean(vc * vc, axis=-1, keepdims=True)
    vln = ((vc * lax.rsqrt(var + EPS)) * lng_ref[...] + lnb_ref[...]).astype(_BF16)
    pi = lax.broadcasted_iota(jnp.int32, (GM_BLOCK, GM_BLOCK), 0) // STREAM_CHUNK
    pj = lax.broadcasted_iota(jnp.int32, (GM_BLOCK, GM_BLOCK), 1) // STREAM_CHUNK
    causal = pi >= pj
    wms = [jnp.where(causal, ws_ref[hh], 0.0).astype(_BF16) for hh in range(GM_HEADS)]
    blocks = []
    for n in range(tm // GM_BLOCK):
        heads = []
        for hh in range(GM_HEADS):
            vb = vln[n * GM_BLOCK:(n + 1) * GM_BLOCK, hh * GM_HEAD_DIM:(hh + 1) * GM_HEAD_DIM]
            heads.append(_dot(wms[hh], vb) + bst_ref[:, hh:hh + 1])
        blocks.append(jnp.concatenate(heads, axis=1))
    mixed = jnp.concatenate(blocks, axis=0)
    a = (ug * mixed).astype(_BF16)
    ya = _dot(a, wgm_ref[...])

    qfio = _dot(h, win_ref[:, COL_HG:COL_HG + 4 * HG_DIM])
    q = qfio[:, 0:HG_DIM]
    f = qfio[:, HG_DIM:2 * HG_DIM]
    logits = lbl_ref[...]
    e = jnp.exp(logits - jnp.max(logits, axis=0, keepdims=True))
    p = e / jnp.sum(e, axis=0, keepdims=True)
    lb = jnp.zeros((1, HG_DIM), _F32)
    for j in range(1, layer + 1):
        lb = lb + p[j:j + 1, :]
    lb = jnp.clip(lb, 0.0, LB_MAX)
    q_scr[...] = q * _sigmoid(q)
    forget = lb + (1.0 - lb) * _sigmoid(f)
    g_scr[...] = jnp.log(jnp.maximum(forget, TINY))
    k_scr[...] = (1.0 - lb) * _sigmoid(-f)
    v_scr[...] = qfio[:, 2 * HG_DIM:3 * HG_DIM]
    og = qfio[:, 3 * HG_DIM:4 * HG_DIM]
    ogs = og * _sigmoid(og)

    c = HG_CHUNK
    for ci in range(tm // c):
        rows = slice(ci * c, (ci + 1) * c)
        g = g_scr[rows, :]
        g_hi = g.astype(_BF16)
        g_r1 = g - g_hi.astype(_F32)
        g_mid = g_r1.astype(_BF16)
        g_lo = (g_r1 - g_mid.astype(_F32)).astype(_BF16)
        sums = _dot(p_ref[...], jnp.concatenate([g_hi, g_mid, g_lo], axis=0))
        gc_scr[rows, :] = sums[0:c, :]
        for lv in range(HG_LEVELS):
            e_scr[lv, rows, :] = jnp.exp(sums[(1 + lv) * c:(2 + lv) * c, :])

    for hh in range(HG_HEADS):
        hs = slice(hh * HG_DK, (hh + 1) * HG_DK)
        st = st_ref[hh]
        for ci in range(tm // c):
            rows = slice(ci * c, (ci + 1) * c)
            gcum = gc_scr[rows, hs]
            glast = gcum[c - 1:c, :]
            kd = (k_scr[rows, hs] * jnp.exp(glast - gcum)).astype(_BF16)
            stt_scr[ci, hh] = st.astype(_BF16).T
            st = st * jnp.exp(glast) + _dot_tn(v_scr[rows, hs].astype(_BF16), kd)
        st_ref[hh] = st

    for ci in range(tm // c):
        rows = slice(ci * c, (ci + 1) * c)
        for hh in range(HG_HEADS):
            hs = slice(hh * HG_DK, (hh + 1) * HG_DK)
            qh = q_scr[rows, hs]
            kh = k_scr[rows, hs]
            vh = v_scr[rows, hs].astype(_BF16)
            w = msk_ref[HG_LEVELS] * jnp.sum(qh * kh, axis=-1, keepdims=True)
            for lv in range(0, HG_LEVELS, 2):
                e0 = e_scr[lv, rows, hs]
                e1 = e_scr[lv + 1, rows, hs]
                qp = jnp.concatenate([qh * e0, qh * e1], axis=0).astype(_BF16)
                kp = jnp.concatenate([kh * e0, kh * e1], axis=0).astype(_BF16)
                kt_scr[ci % 2, hh, lv // 2] = kp.T
                r = _dot(qp, kt_scr[ci % 2, hh, lv // 2])
                w = w + msk_ref[lv] * r[0:c, :] + msk_ref[lv + 1] * r[c:2 * c, :]
            qe = (qh * jnp.exp(gc_scr[rows, hs])).astype(_BF16)
            o = _dot(w.astype(_BF16), jnp.concatenate([vh, vh], axis=0)) + _dot(qe, stt_scr[ci, hh])
            b_scr[rows, hs] = _rms(o, hgn_ref[...])

    b = (b_scr[...] * ogs).astype(_BF16)
    yb = _dot(b, whg_ref[...])

    gates = _dot(h, win_ref[:, COL_GATES:COL_GATES + 2 * D_MODEL])
    y = _sigmoid(gates[:, :D_MODEL]) * ya + _sigmoid(gates[:, D_MODEL:]) * yb
    o_ref[...] = x + _dot(y.astype(_BF16), wout_ref[...])


def _ffn_kernel(tm, final,
                x_ref, nrm_ref, wup_ref, cw_ref, cb_ref, wdn_ref, fin_ref,
                o_ref,
                carry_ref):
    @pl.when(pl.program_id(1) == 0)
    def _():
        carry_ref[...] = jnp.zeros_like(carry_ref)

    x = x_ref[...]
    h = _rms(x, nrm_ref[...]).astype(_BF16)

    def shifted(z, prev, n):
        head = pltpu.roll(jnp.concatenate([prev, z[0:CARRY_ROWS, :]], axis=0), n, axis=0)
        return jnp.concatenate([head[CARRY_ROWS:, :], pltpu.roll(z, n, axis=0)[CARRY_ROWS:, :]], axis=0)

    def conv(col0, w):
        cols = slice(col0, col0 + w)
        z = _dot(h, wup_ref[:, cols])
        prev = carry_ref[:, cols]
        carry_ref[:, cols] = z[tm - CARRY_ROWS:tm, :]
        cw = cw_ref[:, cols]
        return (cb_ref[:, cols] + cw[0:1, :] * shifted(z, prev, 2) + cw[1:2, :] * shifted(z, prev, 1)
                + cw[2:3, :] * z)

    acc = x
    col0 = 0
    for w in FFN_COL_BLOCKS:
        gate = conv(col0, w)
        val = conv(D_FF + col0, w)
        act = (gate * _sigmoid(gate) * val).astype(_BF16)
        acc = acc + _dot(act, wdn_ref[col0:col0 + w, :])
        col0 += w
    if final:
        acc = _rms(acc, fin_ref[...])
    o_ref[...] = acc


def _const_spec(shape):
    nd = len(shape)
    return pl.BlockSpec(shape, lambda b, s: (0,) * nd)


def _mixer_call(layer, n_batch, n_seq, x2, nrm, win, lng, lnb, ws, bst, lbl, hgn, wgm, whg, wout, pmat, masks):
    tm = TM_MIX
    ns = n_seq // tm
    row_spec = pl.BlockSpec((tm, D_MODEL), lambda b, s: (b * ns + s, 0))
    consts = (nrm, win, lng, lnb, ws, bst, lbl, hgn, wgm, whg, wout, pmat, masks)
    return pl.pallas_call(
        functools.partial(_mixer_kernel, layer, tm),
        grid=(n_batch, ns),
        in_specs=[row_spec] + [_const_spec(c.shape) for c in consts],
        out_specs=row_spec,
        out_shape=jax.ShapeDtypeStruct(x2.shape, _F32),
        scratch_shapes=[
            pltpu.VMEM((HG_HEADS, HG_DV, HG_DK), _F32),
            pltpu.VMEM((tm, HG_DIM), _F32),
            pltpu.VMEM((tm, HG_DIM), _F32),
            pltpu.VMEM((tm, HG_DIM), _F32),
            pltpu.VMEM((tm, HG_DIM), _F32),
            pltpu.VMEM((tm, HG_DIM), _F32),
            pltpu.VMEM((tm, HG_DIM), _F32),
            pltpu.VMEM((HG_LEVELS, tm, HG_DIM), _F32),
            pltpu.VMEM((2, HG_HEADS, HG_LEVELS // 2, HG_DK, 2 * HG_CHUNK), _BF16),
            pltpu.VMEM((tm // HG_CHUNK, HG_HEADS, HG_DK, HG_DV), _BF16),
        ],
        compiler_params=pltpu.CompilerParams(
            dimension_semantics=("arbitrary", "arbitrary"),
            vmem_limit_bytes=VMEM_LIMIT_BYTES),
        name=f"mixer_l{layer}",
    )(x2, *consts)


def _ffn_call(layer, final, n_batch, n_seq, x2, nrm, wup, cw, cb, wdn, fin):
    tm = TM_FFN
    ns = n_seq // tm
    row_spec = pl.BlockSpec((tm, D_MODEL), lambda b, s: (b * ns + s, 0))
    consts = (nrm, wup, cw, cb, wdn, fin)
    return pl.pallas_call(
        functools.partial(_ffn_kernel, tm, final),
        grid=(n_batch, ns),
        in_specs=[row_spec] + [_const_spec(c.shape) for c in consts],
        out_specs=row_spec,
        out_shape=jax.ShapeDtypeStruct(x2.shape, _F32),
        scratch_shapes=[
            pltpu.VMEM((CARRY_ROWS, 2 * D_FF), _F32),
        ],
        compiler_params=pltpu.CompilerParams(
            dimension_semantics=("arbitrary", "arbitrary"),
            vmem_limit_bytes=VMEM_LIMIT_BYTES),
        name=f"ffn_l{layer}",
    )(x2, *consts)


def kernel(x, mix_norm, w_in, gm_ln_g, gm_ln_b, gm_ws, gm_bs, hg_lb_logits, hg_norm_g, w_br_gm, w_br_hg,
           w_out, ffn_norm, w_up, conv_w, conv_b, w_down, final_norm):
    n_batch, n_seq, d = x.shape
    depth = w_in.shape[0]
    assert d == D_MODEL and w_in.shape[2] == IN_COLS
    assert n_seq % TM_MIX == 0 and n_seq % TM_FFN == 0 and TM_MIX % GM_BLOCK == 0 and TM_MIX % HG_CHUNK == 0
    assert sum(FFN_COL_BLOCKS) == D_FF and all(w % MXU_DIM == 0 for w in FFN_COL_BLOCKS)
    pmat = jnp.asarray(_range_sum_matrix(HG_CHUNK), _BF16)
    masks = jnp.asarray(_level_masks(HG_CHUNK), _F32)
    x2 = x.reshape(n_batch * n_seq, d)
    row = lambda v: v.reshape(1, -1)
    for l in range(depth):
        x2 = _mixer_call(
            l, n_batch, n_seq, x2, row(mix_norm[l]), w_in[l].astype(_BF16), row(gm_ln_g[l]), row(gm_ln_b[l]),
            gm_ws[l], gm_bs[l].T, hg_lb_logits, row(hg_norm_g[l]),
            w_br_gm[l].astype(_BF16), w_br_hg[l].astype(_BF16), w_out[l].astype(_BF16), pmat, masks)
        x2 = _ffn_call(
            l, l == depth - 1, n_batch, n_seq, x2, row(ffn_norm[l]), w_up[l].astype(_BF16), conv_w[l],
            row(conv_b[l]), w_down[l].astype(_BF16), row(final_norm))
    return x2.reshape(n_batch, n_seq, d)
```

```python
import functools

import numpy as np
import jax
import jax.numpy as jnp
from jax import lax
from jax.experimental import pallas as pl
from jax.experimental.pallas import tpu as pltpu

D_MODEL = 1024
STREAM_CHUNK = 64
GM_BLOCK = 128
GM_HEADS = 4
GM_HEAD_DIM = 128
GM_DIM = GM_HEADS * GM_HEAD_DIM
HG_HEADS = 4
HG_DK = 128
HG_DV = 128
HG_DIM = HG_HEADS * HG_DK
D_FF = 2816
CONV_W = 3
EPS = 1e-6
TINY = 1e-30
LB_MAX = 0.999

COL_UV = 0
COL_HG = 2 * GM_DIM
COL_GATES = COL_HG + 4 * HG_DIM
IN_COLS = COL_GATES + 2 * D_MODEL

HG_CHUNK = 64
HG_LEVELS = 6
SUBLANES = 8
HG_WIDE_LEVELS = 3
TM_MIX = 512
TM_FFN = 512
MXU_DIM = 256
FFN_COL_BLOCKS = (1536, 1280)
CARRY_ROWS = 8
VMEM_LIMIT_BYTES = 56 * 1024 * 1024

_F32 = jnp.float32
_BF16 = jnp.bfloat16


def _range_sum_matrix(c, first_level):
    t = np.arange(c)
    mats = [t[None, :] <= t[:, None]]
    s = (c // 2) >> first_level
    while s >= 1:
        x = np.zeros((c, c), bool)
        for i in range(c):
            base = (i // (2 * s)) * 2 * s
            mid = base + s - 1
            if i - base >= s:
                x[i, mid + 1:i + 1] = True
            else:
                x[i, i + 1:mid + 1] = True
        mats.append(x)
        s //= 2
    p = np.concatenate(mats, 0).astype(np.float32)
    return np.concatenate([p, p, p], 1)


def _level_masks(c):
    i = np.arange(c)[:, None]
    j = np.arange(c)[None, :]
    out = []
    s = c // 2
    while s >= 1:
        out.append((i // (2 * s) == j // (2 * s)) & (i % (2 * s) >= s) & (j % (2 * s) < s))
        s //= 2
    out.append(i == j)
    wide = np.zeros((len(out), c, 2 * c), np.float32)
    for n, m in enumerate(out):
        wide[n, :, (n % 2) * c:(n % 2 + 1) * c] = m
    return wide


def _rms(x, gain):
    r = lax.rsqrt(jnp.mean(x * x, axis=-1, keepdims=True) + EPS)
    return (x * r) * gain


LOG2E = 1.4426950408889634


def _sigmoid(x):
    return 1.0 / (1.0 + jnp.exp2(x * (-LOG2E)))


def _gelu_exact(x):
    return 0.5 * x * (1.0 + lax.erf(x * np.float32(np.sqrt(0.5))))


def _dot(a, b):
    return jnp.dot(a, b, preferred_element_type=_F32)


def _dot_nt(a, b):
    return lax.dot_general(a, b, (((1,), (1,)), ((), ())), preferred_element_type=_F32)


def _dot_tn(a, b):
    return lax.dot_general(a, b, (((0,), (0,)), ((), ())), preferred_element_type=_F32)


def _mixer_kernel(layer, tm,
                  x_ref, nrm_ref, win_ref, lng_ref, lnb_ref, ws_ref, bst_ref, lbl_ref, hgn_ref,
                  wgm_ref, whg_ref, wout_ref, p_ref, msk_ref,
                  o_ref,
                  st_ref, q_scr, k_scr, g_scr, v_scr, b_scr, gc_scr, e_scr, kt_scr, stt_scr, upd_scr,
                  qp_scr, w_scr, qb_scr, kb_scr):
    @pl.when(pl.program_id(1) == 0)
    def _():
        st_ref[...] = jnp.zeros_like(st_ref)

    x = x_ref[...]
    h = _rms(x, nrm_ref[...]).astype(_BF16)

    uv = _dot(h, win_ref[:, COL_UV:COL_UV + 2 * GM_DIM])
    qfio = _dot(h, win_ref[:, COL_HG:COL_HG + 4 * HG_DIM])

    ug = _gelu_exact(uv[:, :GM_DIM])
    vg = _gelu_exact(uv[:, GM_DIM:])
    mu = jnp.mean(vg, axis=-1, keepdims=True)
    vc = vg - mu
    var = jnp.mean(vc * vc, axis=-1, keepdims=True)
    vln = ((vc * lax.rsqrt(var + EPS)) * lng_ref[...] + lnb_ref[...]).astype(_BF16)
    gates = _dot(h, win_ref[:, COL_GATES:COL_GATES + 2 * D_MODEL])

    q = qfio[:, 0:HG_DIM]
    f = qfio[:, HG_DIM:2 * HG_DIM]
    logits = lbl_ref[...]
    e = jnp.exp(logits - jnp.max(logits, axis=0, keepdims=True))
    p = e / jnp.sum(e, axis=0, keepdims=True)
    lb = jnp.zeros((1, HG_DIM), _F32)
    for j in range(1, layer + 1):
        lb = lb + p[j:j + 1, :]
    lb = jnp.clip(lb, 0.0, LB_MAX)
    qt = q * _sigmoid(q)
    q_scr[...] = qt
    qb_scr[...] = qt.astype(_BF16)
    sig_f = _sigmoid(f)
    g_scr[...] = jnp.log(jnp.maximum(lb + (1.0 - lb) * sig_f, TINY)) * LOG2E
    kk = (1.0 - lb) * (1.0 - sig_f)
    k_scr[...] = kk
    kb_scr[...] = kk.astype(_BF16)
    v_scr[...] = qfio[:, 2 * HG_DIM:3 * HG_DIM]
    og = qfio[:, 3 * HG_DIM:4 * HG_DIM]
    ogs = og * _sigmoid(og)

    pi = lax.broadcasted_iota(jnp.int32, (GM_BLOCK, GM_BLOCK), 0) // STREAM_CHUNK
    pj = lax.broadcasted_iota(jnp.int32, (GM_BLOCK, GM_BLOCK), 1) // STREAM_CHUNK
    causal = pi >= pj
    wms = [jnp.where(causal, ws_ref[hh], 0.0).astype(_BF16) for hh in range(GM_HEADS)]
    blocks = []
    for n in range(tm // GM_BLOCK):
        heads = []
        for hh in range(GM_HEADS):
            vb = vln[n * GM_BLOCK:(n + 1) * GM_BLOCK, hh * GM_HEAD_DIM:(hh + 1) * GM_HEAD_DIM]
            heads.append(_dot(wms[hh], vb) + bst_ref[:, hh:hh + 1])
        blocks.append(jnp.concatenate(heads, axis=1))
    mixed = jnp.concatenate(blocks, axis=0)
    a = (ug * mixed).astype(_BF16)
    ya = _sigmoid(gates[:, :D_MODEL]) * _dot(a, wgm_ref[...])
    sgb = _sigmoid(gates[:, D_MODEL:])

    c = HG_CHUNK
    for ci in range(tm // c):
        rows = slice(ci * c, (ci + 1) * c)
        g = g_scr[rows, :]
        g_hi = g.astype(_BF16)
        g_r1 = g - g_hi.astype(_F32)
        g_mid = g_r1.astype(_BF16)
        g_lo = (g_r1 - g_mid.astype(_F32)).astype(_BF16)
        sums = _dot(p_ref[...], jnp.concatenate([g_hi, g_mid, g_lo], axis=0))
        gcum = sums[0:c, :]
        gc_scr[rows, :] = gcum
        for lv in range(HG_WIDE_LEVELS):
            s = c >> (lv + 1)
            pieces = []
            for base in range(0, c, 2 * s):
                gmid = gcum[base + s - 1:base + s, :]
                pieces += [gmid - gcum[base:base + s, :], gcum[base + s:base + 2 * s, :] - gmid]
            e_scr[lv, rows, :] = jnp.exp2(jnp.concatenate(pieces, axis=0)).astype(_BF16)
        for n in range(HG_LEVELS - HG_WIDE_LEVELS):
            e_scr[HG_WIDE_LEVELS + n, rows, :] = jnp.exp2(sums[(1 + n) * c:(2 + n) * c, :]).astype(_BF16)

    for ci in range(tm // c):
        rows = slice(ci * c, (ci + 1) * c)
        for hh in range(HG_HEADS):
            hs = slice(hh * HG_DK, (hh + 1) * HG_DK)
            gcum = gc_scr[rows, hs]
            kd = (k_scr[rows, hs] * jnp.exp2(gcum[c - 1:c, :] - gcum)).astype(_BF16)
            upd_scr[ci, hh] = _dot_tn(v_scr[rows, hs].astype(_BF16), kd)
    for hh in range(HG_HEADS):
        hs = slice(hh * HG_DK, (hh + 1) * HG_DK)
        st = st_ref[hh]
        for ci in range(tm // c):
            stt_scr[ci, hh] = st.astype(_BF16).T
            st = st * jnp.exp2(gc_scr[(ci + 1) * c - 1:(ci + 1) * c, hs]) + upd_scr[ci, hh]
        st_ref[hh] = st

    for ci in range(tm // c):
        rows = slice(ci * c, (ci + 1) * c)
        for hh in range(HG_HEADS):
            hs = slice(hh * HG_DK, (hh + 1) * HG_DK)
            qh = qb_scr[rows, hs]
            kh = kb_scr[rows, hs]
            for lv in range(0, HG_LEVELS, 2):
                e0 = e_scr[lv, rows, hs]
                e1 = e_scr[lv + 1, rows, hs]
                qp_scr[ci, hh, lv // 2] = jnp.concatenate([qh * e0, qh * e1], axis=0)
                kt_scr[ci, hh, lv // 2] = jnp.concatenate([kh * e0, kh * e1], axis=0).T

    for ci in range(tm // c):
        rows = slice(ci * c, (ci + 1) * c)
        for hh in range(HG_HEADS):
            hs = slice(hh * HG_DK, (hh + 1) * HG_DK)
            w = msk_ref[HG_LEVELS] * jnp.sum(q_scr[rows, hs] * k_scr[rows, hs], axis=-1, keepdims=True)
            for lv in range(0, HG_LEVELS, 2):
                r = _dot(qp_scr[ci, hh, lv // 2], kt_scr[ci, hh, lv // 2])
                w = w + msk_ref[lv] * r[0:c, :] + msk_ref[lv + 1] * r[c:2 * c, :]
            w_scr[rows, hs] = w.astype(_BF16)

    for ci in range(tm // c):
        rows = slice(ci * c, (ci + 1) * c)
        for hh in range(HG_HEADS):
            hs = slice(hh * HG_DK, (hh + 1) * HG_DK)
            vh = v_scr[rows, hs].astype(_BF16)
            qe = (q_scr[rows, hs] * jnp.exp2(gc_scr[rows, hs])).astype(_BF16)
            o = _dot(w_scr[rows, hs], jnp.concatenate([vh, vh], axis=0)) + _dot(qe, stt_scr[ci, hh])
            b_scr[rows, hs] = _rms(o, hgn_ref[...])

    b = (b_scr[...] * ogs).astype(_BF16)
    yb = _dot(b, whg_ref[...])

    y = ya + sgb * yb
    o_ref[...] = x + _dot(y.astype(_BF16), wout_ref[...])


def _ffn_kernel(tm, final,
                x_ref, nrm_ref, wup_ref, cw_ref, cb_ref, wdn_ref, fin_ref,
                o_ref,
                carry_ref):
    @pl.when(pl.program_id(1) == 0)
    def _():
        carry_ref[...] = jnp.zeros_like(carry_ref)

    x = x_ref[...]
    h = _rms(x, nrm_ref[...]).astype(_BF16)

    def shifted(z, prev, n):
        head = pltpu.roll(jnp.concatenate([prev, z[0:CARRY_ROWS, :]], axis=0), n, axis=0)
        return jnp.concatenate([head[CARRY_ROWS:, :], pltpu.roll(z, n, axis=0)[CARRY_ROWS:, :]], axis=0)

    def conv(col0, w):
        cols = slice(col0, col0 + w)
        z = _dot(h, wup_ref[:, cols])
        prev = carry_ref[:, cols]
        carry_ref[:, cols] = z[tm - CARRY_ROWS:tm, :]
        cw = cw_ref[:, cols]
        return (cb_ref[:, cols] + cw[0:1, :] * shifted(z, prev, 2) + cw[1:2, :] * shifted(z, prev, 1)
                + cw[2:3, :] * z)

    acc = x
    col0 = 0
    for w in FFN_COL_BLOCKS:
        gate = conv(col0, w)
        val = conv(D_FF + col0, w)
        act = (gate * _sigmoid(gate) * val).astype(_BF16)
        acc = acc + _dot(act, wdn_ref[col0:col0 + w, :])
        col0 += w
    if final:
        acc = _rms(acc, fin_ref[...])
    o_ref[...] = acc


def _const_spec(shape):
    nd = len(shape)
    return pl.BlockSpec(shape, lambda b, s: (0,) * nd)


def _mixer_call(layer, n_batch, n_seq, x2, nrm, win, lng, lnb, ws, bst, lbl, hgn, wgm, whg, wout, pmat, masks):
    tm = TM_MIX
    ns = n_seq // tm
    row_spec = pl.BlockSpec((tm, D_MODEL), lambda b, s: (b * ns + s, 0))
    consts = (nrm, win, lng, lnb, ws, bst, lbl, hgn, wgm, whg, wout, pmat, masks)
    return pl.pallas_call(
        functools.partial(_mixer_kernel, layer, tm),
        grid=(n_batch, ns),
        in_specs=[row_spec] + [_const_spec(c.shape) for c in consts],
        out_specs=row_spec,
        out_shape=jax.ShapeDtypeStruct(x2.shape, _F32),
        scratch_shapes=[
            pltpu.VMEM((HG_HEADS, HG_DV, HG_DK), _F32),
            pltpu.VMEM((tm, HG_DIM), _F32),
            pltpu.VMEM((tm, HG_DIM), _F32),
            pltpu.VMEM((tm, HG_DIM), _F32),
            pltpu.VMEM((tm, HG_DIM), _F32),
            pltpu.VMEM((tm, HG_DIM), _F32),
            pltpu.VMEM((tm, HG_DIM), _F32),
            pltpu.VMEM((HG_LEVELS, tm, HG_DIM), _BF16),
            pltpu.VMEM((tm // HG_CHUNK, HG_HEADS, HG_LEVELS // 2, HG_DK, 2 * HG_CHUNK), _BF16),
            pltpu.VMEM((tm // HG_CHUNK, HG_HEADS, HG_DK, HG_DV), _BF16),
            pltpu.VMEM((tm // HG_CHUNK, HG_HEADS, HG_DV, HG_DK), _F32),
            pltpu.VMEM((tm // HG_CHUNK, HG_HEADS, HG_LEVELS // 2, 2 * HG_CHUNK, HG_DK), _BF16),
            pltpu.VMEM((tm, HG_DIM), _BF16),
            pltpu.VMEM((tm, HG_DIM), _BF16),
            pltpu.VMEM((tm, HG_DIM), _BF16),
        ],
        compiler_params=pltpu.CompilerParams(
            dimension_semantics=("arbitrary", "arbitrary"),
            vmem_limit_bytes=VMEM_LIMIT_BYTES),
        name=f"mixer_l{layer}",
    )(x2, *consts)


def _ffn_call(layer, final, n_batch, n_seq, x2, nrm, wup, cw, cb, wdn, fin):
    tm = TM_FFN
    ns = n_seq // tm
    row_spec = pl.BlockSpec((tm, D_MODEL), lambda b, s: (b * ns + s, 0))
    consts = (nrm, wup, cw, cb, wdn, fin)
    return pl.pallas_call(
        functools.partial(_ffn_kernel, tm, final),
        grid=(n_batch, ns),
        in_specs=[row_spec] + [_const_spec(c.shape) for c in consts],
        out_specs=row_spec,
        out_shape=jax.ShapeDtypeStruct(x2.shape, _F32),
        scratch_shapes=[
            pltpu.VMEM((CARRY_ROWS, 2 * D_FF), _F32),
        ],
        compiler_params=pltpu.CompilerParams(
            dimension_semantics=("arbitrary", "arbitrary"),
            vmem_limit_bytes=VMEM_LIMIT_BYTES),
        name=f"ffn_l{layer}",
    )(x2, *consts)


def kernel(x, mix_norm, w_in, gm_ln_g, gm_ln_b, gm_ws, gm_bs, hg_lb_logits, hg_norm_g, w_br_gm, w_br_hg,
           w_out, ffn_norm, w_up, conv_w, conv_b, w_down, final_norm):
    n_batch, n_seq, d = x.shape
    depth = w_in.shape[0]
    assert d == D_MODEL and w_in.shape[2] == IN_COLS
    assert n_seq % TM_MIX == 0 and n_seq % TM_FFN == 0 and TM_MIX % GM_BLOCK == 0 and TM_MIX % HG_CHUNK == 0
    assert sum(FFN_COL_BLOCKS) == D_FF and all(w % MXU_DIM == 0 for w in FFN_COL_BLOCKS)
    assert (HG_CHUNK >> HG_WIDE_LEVELS) % SUBLANES == 0 and HG_LEVELS % 2 == 0
    pmat = jnp.asarray(_range_sum_matrix(HG_CHUNK, HG_WIDE_LEVELS), _BF16)
    masks = jnp.asarray(_level_masks(HG_CHUNK), _F32)
    x2 = x.reshape(n_batch * n_seq, d)
    row = lambda v: v.reshape(1, -1)
    for l in range(depth):
        x2 = _mixer_call(
            l, n_batch, n_seq, x2, row(mix_norm[l]), w_in[l].astype(_BF16), row(gm_ln_g[l]), row(gm_ln_b[l]),
            gm_ws[l], gm_bs[l].T, hg_lb_logits, row(hg_norm_g[l]),
            w_br_gm[l].astype(_BF16), w_br_hg[l].astype(_BF16), w_out[l].astype(_BF16), pmat, masks)
        x2 = _ffn_call(
            l, l == depth - 1, n_batch, n_seq, x2, row(ffn_norm[l]), w_up[l].astype(_BF16), conv_w[l],
            row(conv_b[l]), w_down[l].astype(_BF16), row(final_norm))
    return x2.reshape(n_batch, n_seq, d)
```

```python
import functools

import numpy as np
import jax
import jax.numpy as jnp
from jax import lax
from jax.experimental import pallas as pl
from jax.experimental.pallas import tpu as pltpu

D_MODEL = 1024
STREAM_CHUNK = 64
GM_BLOCK = 128
GM_HEADS = 4
GM_HEAD_DIM = 128
GM_DIM = GM_HEADS * GM_HEAD_DIM
HG_HEADS = 4
HG_DK = 128
HG_DV = 128
HG_DIM = HG_HEADS * HG_DK
D_FF = 2816
CONV_W = 3
EPS = 1e-6
TINY = 1e-30
LB_MAX = 0.999

COL_UV = 0
COL_HG = 2 * GM_DIM
COL_GATES = COL_HG + 4 * HG_DIM
IN_COLS = COL_GATES + 2 * D_MODEL

HG_CHUNK = 64
HG_LEVELS = 6
SUBLANES = 8
BF16_TILE_ROWS = 16
HG_WIDE_LEVELS = 3
TM_MIX = 512
TM_FFN = 512
MXU_DIM = 256
FFN_COL_BLOCKS = (1536, 1280)
CARRY_ROWS = 8
VMEM_LIMIT_BYTES = 56 * 1024 * 1024

_F32 = jnp.float32
_BF16 = jnp.bfloat16


def _range_sum_matrix(c, first_level):
    t = np.arange(c)
    mats = [t[None, :] <= t[:, None]]
    s = (c // 2) >> first_level
    while s >= 1:
        x = np.zeros((c, c), bool)
        for i in range(c):
            base = (i // (2 * s)) * 2 * s
            mid = base + s - 1
            if i - base >= s:
                x[i, mid + 1:i + 1] = True
            else:
                x[i, i + 1:mid + 1] = True
        mats.append(x)
        s //= 2
    p = np.concatenate(mats, 0).astype(np.float32)
    return np.concatenate([p, p, p], 1)


def _level_masks(c):
    i = np.arange(c)[:, None]
    j = np.arange(c)[None, :]
    out = []
    s = c // 2
    while s >= 1:
        out.append((i // (2 * s) == j // (2 * s)) & (i % (2 * s) >= s) & (j % (2 * s) < s))
        s //= 2
    out.append(i == j)
    wide = np.zeros((len(out), c, 2 * c), np.float32)
    for n, m in enumerate(out):
        wide[n, :, (n % 2) * c:(n % 2 + 1) * c] = m
    return wide


def _rms(x, gain):
    r = lax.rsqrt(jnp.mean(x * x, axis=-1, keepdims=True) + EPS)
    return (x * r) * gain


LOG2E = 1.4426950408889634


def _sigmoid(x):
    return 1.0 / (1.0 + jnp.exp2(x * (-LOG2E)))


def _gelu_exact(x):
    return 0.5 * x * (1.0 + lax.erf(x * np.float32(np.sqrt(0.5))))


def _dot(a, b):
    return jnp.dot(a, b, preferred_element_type=_F32)


def _dot_tn(a, b):
    return lax.dot_general(a, b, (((0,), (0,)), ((), ())), preferred_element_type=_F32)


def _cast_slabs(f32_refs, bf16_refs):
    for src, dst in zip(f32_refs, bf16_refs, strict=True):
        dst[...] = src[...].astype(_BF16)


def _mixer_kernel(layer, tm,
                  x_ref, nrm_ref, win_ref, lng_ref, lnb_ref, ws_ref, bst_ref, lbl_ref, hgn_ref,
                  wgm_ref, whg_ref, wout_ref, p_ref, msk_ref, wup_f32_ref, wdn_f32_ref,
                  o_ref, wup_bf16_ref, wdn_bf16_ref,
                  st_ref, q_scr, k_scr, g_scr, v_scr, b_scr, gc_scr, e_scr, kt_scr, stt_scr, upd_scr,
                  qp_scr, w_scr, qb_scr, kb_scr):
    @pl.when(pl.program_id(1) == 0)
    def _():
        st_ref[...] = jnp.zeros_like(st_ref)

    _cast_slabs((wup_f32_ref, wdn_f32_ref), (wup_bf16_ref, wdn_bf16_ref))

    h = _rms(x_ref[...], nrm_ref[...]).astype(_BF16)

    uv = _dot(h, win_ref[:, COL_UV:COL_UV + 2 * GM_DIM])
    qfio = _dot(h, win_ref[:, COL_HG:COL_HG + 4 * HG_DIM])

    ug = _gelu_exact(uv[:, :GM_DIM])
    vg = _gelu_exact(uv[:, GM_DIM:])
    mu = jnp.mean(vg, axis=-1, keepdims=True)
    vc = vg - mu
    var = jnp.mean(vc * vc, axis=-1, keepdims=True)
    vln = ((vc * lax.rsqrt(var + EPS)) * lng_ref[...] + lnb_ref[...]).astype(_BF16)
    gates = _dot(h, win_ref[:, COL_GATES:COL_GATES + 2 * D_MODEL])

    q = qfio[:, 0:HG_DIM]
    f = qfio[:, HG_DIM:2 * HG_DIM]
    logits = lbl_ref[...]
    e = jnp.exp(logits - jnp.max(logits, axis=0, keepdims=True))
    p = e / jnp.sum(e, axis=0, keepdims=True)
    lb = jnp.zeros((1, HG_DIM), _F32)
    for j in range(1, layer + 1):
        lb = lb + p[j:j + 1, :]
    lb = jnp.clip(lb, 0.0, LB_MAX)
    qt = q * _sigmoid(q)
    q_scr[...] = qt
    qb_scr[...] = qt.astype(_BF16)
    sig_f = _sigmoid(f)
    g_scr[...] = jnp.log(jnp.maximum(lb + (1.0 - lb) * sig_f, TINY)) * LOG2E
    kk = (1.0 - lb) * (1.0 - sig_f)
    k_scr[...] = kk
    kb_scr[...] = kk.astype(_BF16)
    v_scr[...] = qfio[:, 2 * HG_DIM:3 * HG_DIM]
    og = qfio[:, 3 * HG_DIM:4 * HG_DIM]
    ogs = og * _sigmoid(og)

    pi = lax.broadcasted_iota(jnp.int32, (GM_BLOCK, GM_BLOCK), 0) // STREAM_CHUNK
    pj = lax.broadcasted_iota(jnp.int32, (GM_BLOCK, GM_BLOCK), 1) // STREAM_CHUNK
    causal = pi >= pj
    wms = [jnp.where(causal, ws_ref[hh], 0.0).astype(_BF16) for hh in range(GM_HEADS)]
    blocks = []
    for n in range(tm // GM_BLOCK):
        heads = []
        for hh in range(GM_HEADS):
            vb = vln[n * GM_BLOCK:(n + 1) * GM_BLOCK, hh * GM_HEAD_DIM:(hh + 1) * GM_HEAD_DIM]
            heads.append(_dot(wms[hh], vb) + bst_ref[:, hh:hh + 1])
        blocks.append(jnp.concatenate(heads, axis=1))
    mixed = jnp.concatenate(blocks, axis=0)
    a = (ug * mixed).astype(_BF16)
    ya = _sigmoid(gates[:, :D_MODEL]) * _dot(a, wgm_ref[...])
    sgb = _sigmoid(gates[:, D_MODEL:])

    c = HG_CHUNK
    for ci in range(tm // c):
        rows = slice(ci * c, (ci + 1) * c)
        g = g_scr[rows, :]
        g_hi = g.astype(_BF16)
        g_r1 = g - g_hi.astype(_F32)
        g_mid = g_r1.astype(_BF16)
        g_lo = (g_r1 - g_mid.astype(_F32)).astype(_BF16)
        sums = _dot(p_ref[...], jnp.concatenate([g_hi, g_mid, g_lo], axis=0))
        gcum = sums[0:c, :]
        gc_scr[rows, :] = gcum
        for lv in range(HG_WIDE_LEVELS):
            s = c >> (lv + 1)
            pieces = []
            for base in range(0, c, 2 * s):
                gmid = gcum[base + s - 1:base + s, :]
                pieces += [gmid - gcum[base:base + s, :], gcum[base + s:base + 2 * s, :] - gmid]
            e_scr[lv, rows, :] = jnp.exp2(jnp.concatenate(pieces, axis=0)).astype(_BF16)
        for n in range(HG_LEVELS - HG_WIDE_LEVELS):
            e_scr[HG_WIDE_LEVELS + n, rows, :] = jnp.exp2(sums[(1 + n) * c:(2 + n) * c, :]).astype(_BF16)

    for ci in range(tm // c):
        rows = slice(ci * c, (ci + 1) * c)
        for hh in range(HG_HEADS):
            hs = slice(hh * HG_DK, (hh + 1) * HG_DK)
            gcum = gc_scr[rows, hs]
            kd = (k_scr[rows, hs] * jnp.exp2(gcum[c - 1:c, :] - gcum)).astype(_BF16)
            upd_scr[ci, hh] = _dot_tn(v_scr[rows, hs].astype(_BF16), kd)
    for hh in range(HG_HEADS):
        hs = slice(hh * HG_DK, (hh + 1) * HG_DK)
        st = st_ref[hh]
        for ci in range(tm // c):
            stt_scr[ci, hh] = st.astype(_BF16).T
            st = st * jnp.exp2(gc_scr[(ci + 1) * c - 1:(ci + 1) * c, hs]) + upd_scr[ci, hh]
        st_ref[hh] = st

    for ci in range(tm // c):
        rows = slice(ci * c, (ci + 1) * c)
        for hh in range(HG_HEADS):
            hs = slice(hh * HG_DK, (hh + 1) * HG_DK)
            qh = qb_scr[rows, hs]
            kh = kb_scr[rows, hs]
            for lv in range(0, HG_LEVELS, 2):
                e0 = e_scr[lv, rows, hs]
                e1 = e_scr[lv + 1, rows, hs]
                qp_scr[ci, hh, lv // 2] = jnp.concatenate([qh * e0, qh * e1], axis=0)
                kt_scr[ci, hh, lv // 2] = jnp.concatenate([kh * e0, kh * e1], axis=0).T

    for ci in range(tm // c):
        rows = slice(ci * c, (ci + 1) * c)
        for hh in range(HG_HEADS):
            hs = slice(hh * HG_DK, (hh + 1) * HG_DK)
            w = msk_ref[HG_LEVELS] * jnp.sum(q_scr[rows, hs] * k_scr[rows, hs], axis=-1, keepdims=True)
            for lv in range(0, HG_LEVELS, 2):
                r = _dot(qp_scr[ci, hh, lv // 2], kt_scr[ci, hh, lv // 2])
                w = w + msk_ref[lv] * r[0:c, :] + msk_ref[lv + 1] * r[c:2 * c, :]
            w_scr[rows, hs] = w.astype(_BF16)

    for ci in range(tm // c):
        rows = slice(ci * c, (ci + 1) * c)
        for hh in range(HG_HEADS):
            hs = slice(hh * HG_DK, (hh + 1) * HG_DK)
            vh = v_scr[rows, hs].astype(_BF16)
            qe = (q_scr[rows, hs] * jnp.exp2(gc_scr[rows, hs])).astype(_BF16)
            o = _dot(w_scr[rows, hs], jnp.concatenate([vh, vh], axis=0)) + _dot(qe, stt_scr[ci, hh])
            b_scr[rows, hs] = _rms(o, hgn_ref[...])

    b = (b_scr[...] * ogs).astype(_BF16)
    yb = _dot(b, whg_ref[...])

    y = ya + sgb * yb
    o_ref[...] = x_ref[...] + _dot(y.astype(_BF16), wout_ref[...])


def _ffn_kernel(tm, final, n_cast,
                x_ref, nrm_ref, wup_ref, cw_ref, cb_ref, wdn_ref, fin_ref, *rest):
    cast_in, o_ref, cast_out, carry_ref = rest[:n_cast], rest[n_cast], rest[n_cast + 1:-1], rest[-1]

    @pl.when(pl.program_id(1) == 0)
    def _():
        carry_ref[...] = jnp.zeros_like(carry_ref)

    _cast_slabs(cast_in, cast_out)

    h = _rms(x_ref[...], nrm_ref[...]).astype(_BF16)

    def shifted(z, prev, n):
        head = pltpu.roll(jnp.concatenate([prev, z[0:CARRY_ROWS, :]], axis=0), n, axis=0)
        return jnp.concatenate([head[CARRY_ROWS:, :], pltpu.roll(z, n, axis=0)[CARRY_ROWS:, :]], axis=0)

    def conv(col0, w):
        cols = slice(col0, col0 + w)
        z = _dot(h, wup_ref[:, cols])
        prev = carry_ref[:, cols]
        carry_ref[:, cols] = z[tm - CARRY_ROWS:tm, :]
        cw = cw_ref[:, cols]
        return (cb_ref[:, cols] + cw[0:1, :] * shifted(z, prev, 2) + cw[1:2, :] * shifted(z, prev, 1)
                + cw[2:3, :] * z)

    acc = None
    col0 = 0
    for w in FFN_COL_BLOCKS:
        gate = conv(col0, w)
        val = conv(D_FF + col0, w)
        act = (gate * _sigmoid(gate) * val).astype(_BF16)
        down = _dot(act, wdn_ref[col0:col0 + w, :])
        acc = down if acc is None else acc + down
        col0 += w
    acc = x_ref[...] + acc
    if final:
        acc = _rms(acc, fin_ref[...])
    o_ref[...] = acc


def _const_spec(shape):
    nd = len(shape)
    return pl.BlockSpec(shape, lambda b, s: (0,) * nd)


def _cast_specs(stack, layer, n_steps, ns):
    rows, cols = stack.shape[1:]
    group = 1
    while (rows * group) % (n_steps * BF16_TILE_ROWS):
        group *= 2
    slab = rows * group // n_steps
    in_spec = pl.BlockSpec((None, slab, cols), lambda b, s: (layer, (b * ns + s) // group, 0))
    out_spec = pl.BlockSpec((slab, cols), lambda b, s: ((b * ns + s) // group, 0))
    return in_spec, out_spec, jax.ShapeDtypeStruct((rows, cols), _BF16)


def _mixer_call(layer, n_batch, n_seq, x2, consts, next_weights):
    tm = TM_MIX
    ns = n_seq // tm
    row_spec = pl.BlockSpec((tm, D_MODEL), lambda b, s: (b * ns + s, 0))
    casts = [_cast_specs(w, layer, n_batch * ns, ns) for w in next_weights]
    n_chunks = tm // HG_CHUNK
    return pl.pallas_call(
        functools.partial(_mixer_kernel, layer, tm),
        grid=(n_batch, ns),
        in_specs=[row_spec] + [_const_spec(c.shape) for c in consts] + [c[0] for c in casts],
        out_specs=[row_spec] + [c[1] for c in casts],
        out_shape=[jax.ShapeDtypeStruct(x2.shape, _F32)] + [c[2] for c in casts],
        scratch_shapes=[
            pltpu.VMEM((HG_HEADS, HG_DV, HG_DK), _F32),
            pltpu.VMEM((tm, HG_DIM), _F32),
            pltpu.VMEM((tm, HG_DIM), _F32),
            pltpu.VMEM((tm, HG_DIM), _F32),
            pltpu.VMEM((tm, HG_DIM), _F32),
            pltpu.VMEM((tm, HG_DIM), _F32),
            pltpu.VMEM((tm, HG_DIM), _F32),
            pltpu.VMEM((HG_LEVELS, tm, HG_DIM), _BF16),
            pltpu.VMEM((n_chunks, HG_HEADS, HG_LEVELS // 2, HG_DK, 2 * HG_CHUNK), _BF16),
            pltpu.VMEM((n_chunks, HG_HEADS, HG_DK, HG_DV), _BF16),
            pltpu.VMEM((n_chunks, HG_HEADS, HG_DV, HG_DK), _F32),
            pltpu.VMEM((n_chunks, HG_HEADS, HG_LEVELS // 2, 2 * HG_CHUNK, HG_DK), _BF16),
            pltpu.VMEM((tm, HG_DIM), _BF16),
            pltpu.VMEM((tm, HG_DIM), _BF16),
            pltpu.VMEM((tm, HG_DIM), _BF16),
        ],
        compiler_params=pltpu.CompilerParams(
            dimension_semantics=("arbitrary", "arbitrary"),
            vmem_limit_bytes=VMEM_LIMIT_BYTES),
        name=f"mixer_l{layer}",
    )(x2, *consts, *next_weights)


def _ffn_call(layer, final, n_batch, n_seq, x2, consts, next_weights):
    tm = TM_FFN
    ns = n_seq // tm
    row_spec = pl.BlockSpec((tm, D_MODEL), lambda b, s: (b * ns + s, 0))
    casts = [_cast_specs(w, layer + 1, n_batch * ns, ns) for w in next_weights]
    return pl.pallas_call(
        functools.partial(_ffn_kernel, tm, final, len(casts)),
        grid=(n_batch, ns),
        in_specs=[row_spec] + [_const_spec(c.shape) for c in consts] + [c[0] for c in casts],
        out_specs=[row_spec] + [c[1] for c in casts],
        out_shape=[jax.ShapeDtypeStruct(x2.shape, _F32)] + [c[2] for c in casts],
        scratch_shapes=[
            pltpu.VMEM((CARRY_ROWS, 2 * D_FF), _F32),
        ],
        compiler_params=pltpu.CompilerParams(
            dimension_semantics=("arbitrary", "arbitrary"),
            vmem_limit_bytes=VMEM_LIMIT_BYTES),
        name=f"ffn_l{layer}",
    )(x2, *consts, *next_weights)


def kernel(x, mix_norm, w_in, gm_ln_g, gm_ln_b, gm_ws, gm_bs, hg_lb_logits, hg_norm_g, w_br_gm, w_br_hg,
           w_out, ffn_norm, w_up, conv_w, conv_b, w_down, final_norm):
    n_batch, n_seq, d = x.shape
    depth = w_in.shape[0]
    assert d == D_MODEL and w_in.shape[2] == IN_COLS
    assert n_seq % TM_MIX == 0 and n_seq % TM_FFN == 0 and TM_MIX % GM_BLOCK == 0 and TM_MIX % HG_CHUNK == 0
    assert sum(FFN_COL_BLOCKS) == D_FF and all(w % MXU_DIM == 0 for w in FFN_COL_BLOCKS)
    assert (HG_CHUNK >> HG_WIDE_LEVELS) % SUBLANES == 0 and HG_LEVELS % 2 == 0
    pmat = jnp.asarray(_range_sum_matrix(HG_CHUNK, HG_WIDE_LEVELS), _BF16)
    masks = jnp.asarray(_level_masks(HG_CHUNK), _F32)
    x2 = x.reshape(n_batch * n_seq, d)
    row = lambda v: v.reshape(1, -1)
    mixer_stacks = (w_in, w_br_gm, w_br_hg, w_out)
    win_b, wgm_b, whg_b, wout_b = (w[0].astype(_BF16) for w in mixer_stacks)
    for l in range(depth):
        x2, wup_b, wdn_b = _mixer_call(
            l, n_batch, n_seq, x2,
            (row(mix_norm[l]), win_b, row(gm_ln_g[l]), row(gm_ln_b[l]), gm_ws[l], gm_bs[l].T, hg_lb_logits,
             row(hg_norm_g[l]), wgm_b, whg_b, wout_b, pmat, masks),
            (w_up, w_down))
        last = l == depth - 1
        x2, *next_b = _ffn_call(
            l, last, n_batch, n_seq, x2,
            (row(ffn_norm[l]), wup_b, conv_w[l], row(conv_b[l]), wdn_b, row(final_norm)),
            () if last else mixer_stacks)
        if not last:
            win_b, wgm_b, whg_b, wout_b = next_b
    return x2.reshape(n_batch, n_seq, d)
```

```python
import functools

import numpy as np
import jax
import jax.numpy as jnp
from jax import lax
from jax.experimental import pallas as pl
from jax.experimental.pallas import tpu as pltpu

D_MODEL = 1024
STREAM_CHUNK = 64
GM_BLOCK = 128
GM_HEADS = 4
GM_HEAD_DIM = 128
GM_DIM = GM_HEADS * GM_HEAD_DIM
HG_HEADS = 4
HG_DK = 128
HG_DV = 128
HG_DIM = HG_HEADS * HG_DK
D_FF = 2816
CONV_W = 3
EPS = 1e-6
TINY = 1e-30
LB_MAX = 0.999

COL_UV = 0
COL_HG = 2 * GM_DIM
COL_GATES = COL_HG + 4 * HG_DIM
IN_COLS = COL_GATES + 2 * D_MODEL

HG_CHUNK = 64
HG_LEVELS = 6
SUBLANES = 8
LANES = 128
BF16_TILE_ROWS = 16
HG_WIDE_LEVELS = 3
TM_MIX = 512
TM_FFN = 512
MXU_DIM = 256
FFN_COL_BLOCKS = (1536, 1280)
ROW_GROUP = SUBLANES * SUBLANES
CARRY_ROWS = 2 * SUBLANES
VMEM_LIMIT_BYTES = 56 * 1024 * 1024

_F32 = jnp.float32
_BF16 = jnp.bfloat16


def _range_sum_matrix(c, first_level):
    t = np.arange(c)
    mats = [t[None, :] <= t[:, None]]
    s = (c // 2) >> first_level
    while s >= 1:
        x = np.zeros((c, c), bool)
        for i in range(c):
            base = (i // (2 * s)) * 2 * s
            mid = base + s - 1
            if i - base >= s:
                x[i, mid + 1:i + 1] = True
            else:
                x[i, i + 1:mid + 1] = True
        mats.append(x)
        s //= 2
    p = np.concatenate(mats, 0).astype(np.float32)
    return np.concatenate([p, p, p], 1)


def _level_masks(c):
    i = np.arange(c)[:, None]
    j = np.arange(c)[None, :]
    out = []
    s = c // 2
    while s >= 1:
        out.append((i // (2 * s) == j // (2 * s)) & (i % (2 * s) >= s) & (j % (2 * s) < s))
        s //= 2
    out.append(i == j)
    wide = np.zeros((len(out), c, 2 * c), np.float32)
    for n, m in enumerate(out):
        wide[n, :, (n % 2) * c:(n % 2 + 1) * c] = m
    return wide


def _rms(x, gain):
    r = lax.rsqrt(jnp.mean(x * x, axis=-1, keepdims=True) + EPS)
    return (x * r) * gain


LOG2E = 1.4426950408889634


def _sigmoid(x):
    return 1.0 / (1.0 + jnp.exp2(x * (-LOG2E)))


def _gelu_exact(x):
    return 0.5 * x * (1.0 + lax.erf(x * np.float32(np.sqrt(0.5))))


def _dot(a, b):
    return jnp.dot(a, b, preferred_element_type=_F32)


def _dot_tn(a, b):
    return lax.dot_general(a, b, (((0,), (0,)), ((), ())), preferred_element_type=_F32)


def _cast_slabs(f32_refs, bf16_refs):
    for src, dst in zip(f32_refs, bf16_refs, strict=True):
        dst[...] = src[...].astype(_BF16)


def _load_rows(x_ref, lane_major):
    if not lane_major:
        return x_ref[...]
    return jnp.concatenate([x_ref[k] for k in range(D_MODEL // LANES)], axis=1)


def _mixer_kernel(layer, tm, x_lane_major,
                  x_ref, nrm_ref, win_ref, lng_ref, lnb_ref, ws_ref, bst_ref, lbl_ref, hgn_ref,
                  wgm_ref, whg_ref, wout_ref, p_ref, msk_ref, wup_f32_ref, wdn_f32_ref,
                  o_ref, wup_bf16_ref, wdn_bf16_ref,
                  st_ref, q_scr, k_scr, g_scr, v_scr, b_scr, gc_scr, e_scr, kt_scr, stt_scr, upd_scr,
                  qp_scr, w_scr, qb_scr, kb_scr):
    @pl.when(pl.program_id(1) == 0)
    def _():
        st_ref[...] = jnp.zeros_like(st_ref)

    _cast_slabs((wup_f32_ref, wdn_f32_ref), (wup_bf16_ref, wdn_bf16_ref))

    h = _rms(_load_rows(x_ref, x_lane_major), nrm_ref[...]).astype(_BF16)

    uv = _dot(h, win_ref[:, COL_UV:COL_UV + 2 * GM_DIM])
    qfio = _dot(h, win_ref[:, COL_HG:COL_HG + 4 * HG_DIM])

    ug = _gelu_exact(uv[:, :GM_DIM])
    vg = _gelu_exact(uv[:, GM_DIM:])
    mu = jnp.mean(vg, axis=-1, keepdims=True)
    vc = vg - mu
    var = jnp.mean(vc * vc, axis=-1, keepdims=True)
    vln = ((vc * lax.rsqrt(var + EPS)) * lng_ref[...] + lnb_ref[...]).astype(_BF16)

    q = qfio[:, 0:HG_DIM]
    f = qfio[:, HG_DIM:2 * HG_DIM]
    logits = lbl_ref[...]
    e = jnp.exp(logits - jnp.max(logits, axis=0, keepdims=True))
    p = e / jnp.sum(e, axis=0, keepdims=True)
    lb = jnp.zeros((1, HG_DIM), _F32)
    for j in range(1, layer + 1):
        lb = lb + p[j:j + 1, :]
    lb = jnp.clip(lb, 0.0, LB_MAX)
    qt = q * _sigmoid(q)
    q_scr[...] = qt
    qb_scr[...] = qt.astype(_BF16)
    sig_f = _sigmoid(f)
    g_scr[...] = jnp.log(jnp.maximum(lb + (1.0 - lb) * sig_f, TINY)) * LOG2E
    kk = (1.0 - lb) * (1.0 - sig_f)
    k_scr[...] = kk
    kb_scr[...] = kk.astype(_BF16)
    v_scr[...] = qfio[:, 2 * HG_DIM:3 * HG_DIM]
    og = qfio[:, 3 * HG_DIM:4 * HG_DIM]
    ogs = og * _sigmoid(og)

    pi = lax.broadcasted_iota(jnp.int32, (GM_BLOCK, GM_BLOCK), 0) // STREAM_CHUNK
    pj = lax.broadcasted_iota(jnp.int32, (GM_BLOCK, GM_BLOCK), 1) // STREAM_CHUNK
    causal = pi >= pj
    wms = [jnp.where(causal, ws_ref[hh], 0.0).astype(_BF16) for hh in range(GM_HEADS)]
    blocks = []
    for n in range(tm // GM_BLOCK):
        heads = []
        for hh in range(GM_HEADS):
            vb = vln[n * GM_BLOCK:(n + 1) * GM_BLOCK, hh * GM_HEAD_DIM:(hh + 1) * GM_HEAD_DIM]
            heads.append(_dot(wms[hh], vb) + bst_ref[:, hh:hh + 1])
        blocks.append(jnp.concatenate(heads, axis=1))
    mixed = jnp.concatenate(blocks, axis=0)
    a = (ug * mixed).astype(_BF16)
    ya = _dot(a, wgm_ref[...])

    def gate_quarter(n):
        c0 = COL_GATES + n * (D_MODEL // 2)
        return _dot(h, win_ref[:, c0:c0 + D_MODEL // 2])

    c = HG_CHUNK
    for ci in range(tm // c):
        rows = slice(ci * c, (ci + 1) * c)
        g = g_scr[rows, :]
        g_hi = g.astype(_BF16)
        g_r1 = g - g_hi.astype(_F32)
        g_mid = g_r1.astype(_BF16)
        g_lo = (g_r1 - g_mid.astype(_F32)).astype(_BF16)
        sums = _dot(p_ref[...], jnp.concatenate([g_hi, g_mid, g_lo], axis=0))
        gcum = sums[0:c, :]
        gc_scr[rows, :] = gcum
        for lv in range(HG_WIDE_LEVELS):
            s = c >> (lv + 1)
            pieces = []
            for base in range(0, c, 2 * s):
                gmid = gcum[base + s - 1:base + s, :]
                pieces += [gmid - gcum[base:base + s, :], gcum[base + s:base + 2 * s, :] - gmid]
            e_scr[lv, rows, :] = jnp.exp2(jnp.concatenate(pieces, axis=0)).astype(_BF16)
        for n in range(HG_LEVELS - HG_WIDE_LEVELS):
            e_scr[HG_WIDE_LEVELS + n, rows, :] = jnp.exp2(sums[(1 + n) * c:(2 + n) * c, :]).astype(_BF16)

    gq0 = gate_quarter(0)

    for ci in range(tm // c):
        rows = slice(ci * c, (ci + 1) * c)
        for hh in range(HG_HEADS):
            hs = slice(hh * HG_DK, (hh + 1) * HG_DK)
            gcum = gc_scr[rows, hs]
            kd = (k_scr[rows, hs] * jnp.exp2(gcum[c - 1:c, :] - gcum)).astype(_BF16)
            upd_scr[ci, hh] = _dot_tn(v_scr[rows, hs].astype(_BF16), kd)
    for hh in range(HG_HEADS):
        hs = slice(hh * HG_DK, (hh + 1) * HG_DK)
        st = st_ref[hh]
        for ci in range(tm // c):
            stt_scr[ci, hh] = st.astype(_BF16).T
            st = st * jnp.exp2(gc_scr[(ci + 1) * c - 1:(ci + 1) * c, hs]) + upd_scr[ci, hh]
        st_ref[hh] = st

    gq1 = gate_quarter(1)

    for ci in range(tm // c):
        rows = slice(ci * c, (ci + 1) * c)
        for hh in range(HG_HEADS):
            hs = slice(hh * HG_DK, (hh + 1) * HG_DK)
            qh = qb_scr[rows, hs]
            kh = kb_scr[rows, hs]
            for lv in range(0, HG_LEVELS, 2):
                e0 = e_scr[lv, rows, hs]
                e1 = e_scr[lv + 1, rows, hs]
                qp_scr[ci, hh, lv // 2] = jnp.concatenate([qh * e0, qh * e1], axis=0)
                kt_scr[ci, hh, lv // 2] = jnp.concatenate([kh * e0, kh * e1], axis=0).T

    gq2 = gate_quarter(2)

    for ci in range(tm // c):
        rows = slice(ci * c, (ci + 1) * c)
        for hh in range(HG_HEADS):
            hs = slice(hh * HG_DK, (hh + 1) * HG_DK)
            w = msk_ref[HG_LEVELS] * jnp.sum(q_scr[rows, hs] * k_scr[rows, hs], axis=-1, keepdims=True)
            for lv in range(0, HG_LEVELS, 2):
                r = _dot(qp_scr[ci, hh, lv // 2], kt_scr[ci, hh, lv // 2])
                w = w + msk_ref[lv] * r[0:c, :] + msk_ref[lv + 1] * r[c:2 * c, :]
            w_scr[rows, hs] = w.astype(_BF16)

    gq3 = gate_quarter(3)

    for ci in range(tm // c):
        rows = slice(ci * c, (ci + 1) * c)
        for hh in range(HG_HEADS):
            hs = slice(hh * HG_DK, (hh + 1) * HG_DK)
            vh = v_scr[rows, hs].astype(_BF16)
            qe = (q_scr[rows, hs] * jnp.exp2(gc_scr[rows, hs])).astype(_BF16)
            o = _dot(w_scr[rows, hs], jnp.concatenate([vh, vh], axis=0)) + _dot(qe, stt_scr[ci, hh])
            b_scr[rows, hs] = _rms(o, hgn_ref[...])

    b = (b_scr[...] * ogs).astype(_BF16)
    yb = _dot(b, whg_ref[...])

    y = _sigmoid(jnp.concatenate([gq0, gq1], axis=1)) * ya + _sigmoid(jnp.concatenate([gq2, gq3], axis=1)) * yb
    out = _load_rows(x_ref, x_lane_major) + _dot(y.astype(_BF16), wout_ref[...])
    for k in range(D_MODEL // LANES):
        o_ref[k] = out[:, k * LANES:(k + 1) * LANES]


def _ffn_kernel(tm, final, n_cast,
                x_ref, nrm_ref, wup_ref, cw_ref, cb_ref, wdn_ref, fin_ref, *rest):
    cast_in, o_ref, cast_out = rest[:n_cast], rest[n_cast], rest[n_cast + 1:2 * n_cast + 1]
    carry_ref, *final_scr = rest[2 * n_cast + 1:]

    @pl.when(pl.program_id(1) == 0)
    def _():
        carry_ref[...] = jnp.zeros_like(carry_ref)

    _cast_slabs(cast_in, cast_out)

    n_groups = tm // ROW_GROUP
    slabs = [(g * ROW_GROUP + a, g * ROW_GROUP + a * SUBLANES) for g in range(n_groups) for a in range(SUBLANES)]
    n_lane_tiles = D_MODEL // LANES
    xp = jnp.concatenate(
        [jnp.concatenate([x_ref[k, pl.ds(t0, SUBLANES, stride=SUBLANES), :] for k in range(n_lane_tiles)], axis=1)
         for t0, _ in slabs], axis=0)
    h = _rms(xp, nrm_ref[...]).astype(_BF16)

    def wrap_fix(reg, prev_reg):
        sub0 = lax.broadcasted_iota(jnp.int32, reg.shape, 0) == 0
        return jnp.where(sub0, pltpu.roll(prev_reg, 1, axis=0), pltpu.roll(reg, 1, axis=0))

    def conv(col0, w):
        cols = slice(col0, col0 + w)
        z = _dot(h, wup_ref[:, cols])
        prev = carry_ref[:, cols]
        carry_ref[:, cols] = z[tm - CARRY_ROWS:tm, :]
        z1, z2 = [], []
        for g in range(n_groups):
            zg = z[g * ROW_GROUP:(g + 1) * ROW_GROUP, :]
            pg = prev if g == 0 else z[g * ROW_GROUP - CARRY_ROWS:g * ROW_GROUP, :]
            fix1 = wrap_fix(zg[7 * SUBLANES:, :], pg[SUBLANES:, :])
            fix2 = wrap_fix(zg[6 * SUBLANES:7 * SUBLANES, :], pg[:SUBLANES, :])
            z1 += [fix1, zg[:7 * SUBLANES, :]]
            z2 += [fix2, fix1, zg[:6 * SUBLANES, :]]
        cw = cw_ref[:, cols]
        return (cb_ref[:, cols] + cw[0:1, :] * jnp.concatenate(z2, axis=0)
                + cw[1:2, :] * jnp.concatenate(z1, axis=0) + cw[2:3, :] * z)

    acc = None
    col0 = 0
    for w in FFN_COL_BLOCKS:
        gate = conv(col0, w)
        val = conv(D_FF + col0, w)
        act = (gate * _sigmoid(gate) * val).astype(_BF16)
        down = _dot(act, wdn_ref[col0:col0 + w, :])
        acc = down if acc is None else acc + down
        col0 += w
    acc = xp + acc
    if final:
        acc = _rms(acc, fin_ref[...])
    dst = final_scr[0] if final else o_ref
    for t0, r0 in slabs:
        for k in range(n_lane_tiles):
            dst[k, pl.ds(t0, SUBLANES, stride=SUBLANES), :] = acc[r0:r0 + SUBLANES, k * LANES:(k + 1) * LANES]
    if final:
        for k in range(n_lane_tiles):
            o_ref[:, k * LANES:(k + 1) * LANES] = dst[k]


def _const_spec(shape):
    nd = len(shape)
    return pl.BlockSpec(shape, lambda b, s: (0,) * nd)


def _row_major_spec(tm, ns):
    return pl.BlockSpec((tm, D_MODEL), lambda b, s: (b * ns + s, 0))


def _lane_major_spec(tm, ns):
    return pl.BlockSpec((D_MODEL // LANES, tm, LANES), lambda b, s: (0, b * ns + s, 0))


def _cast_specs(stack, layer, n_steps, ns):
    rows, cols = stack.shape[1:]
    group = 1
    while (rows * group) % (n_steps * BF16_TILE_ROWS):
        group *= 2
    slab = rows * group // n_steps
    in_spec = pl.BlockSpec((None, slab, cols), lambda b, s: (layer, (b * ns + s) // group, 0))
    out_spec = pl.BlockSpec((slab, cols), lambda b, s: ((b * ns + s) // group, 0))
    return in_spec, out_spec, jax.ShapeDtypeStruct((rows, cols), _BF16)


def _mixer_call(layer, n_batch, n_seq, x2, consts, next_weights):
    tm = TM_MIX
    ns = n_seq // tm
    x_lane_major = x2.ndim == 3
    in_spec = _lane_major_spec(tm, ns) if x_lane_major else _row_major_spec(tm, ns)
    casts = [_cast_specs(w, layer, n_batch * ns, ns) for w in next_weights]
    n_chunks = tm // HG_CHUNK
    return pl.pallas_call(
        functools.partial(_mixer_kernel, layer, tm, x_lane_major),
        grid=(n_batch, ns),
        in_specs=[in_spec] + [_const_spec(c.shape) for c in consts] + [c[0] for c in casts],
        out_specs=[_lane_major_spec(tm, ns)] + [c[1] for c in casts],
        out_shape=[jax.ShapeDtypeStruct((D_MODEL // LANES, n_batch * n_seq, LANES), _F32)] + [c[2] for c in casts],
        scratch_shapes=[
            pltpu.VMEM((HG_HEADS, HG_DV, HG_DK), _F32),
            pltpu.VMEM((tm, HG_DIM), _F32),
            pltpu.VMEM((tm, HG_DIM), _F32),
            pltpu.VMEM((tm, HG_DIM), _F32),
            pltpu.VMEM((tm, HG_DIM), _F32),
            pltpu.VMEM((tm, HG_DIM), _F32),
            pltpu.VMEM((tm, HG_DIM), _F32),
            pltpu.VMEM((HG_LEVELS, tm, HG_DIM), _BF16),
            pltpu.VMEM((n_chunks, HG_HEADS, HG_LEVELS // 2, HG_DK, 2 * HG_CHUNK), _BF16),
            pltpu.VMEM((n_chunks, HG_HEADS, HG_DK, HG_DV), _BF16),
            pltpu.VMEM((n_chunks, HG_HEADS, HG_DV, HG_DK), _F32),
            pltpu.VMEM((n_chunks, HG_HEADS, HG_LEVELS // 2, 2 * HG_CHUNK, HG_DK), _BF16),
            pltpu.VMEM((tm, HG_DIM), _BF16),
            pltpu.VMEM((tm, HG_DIM), _BF16),
            pltpu.VMEM((tm, HG_DIM), _BF16),
        ],
        compiler_params=pltpu.CompilerParams(
            dimension_semantics=("arbitrary", "arbitrary"),
            vmem_limit_bytes=VMEM_LIMIT_BYTES),
        name=f"mixer_l{layer}",
    )(x2, *consts, *next_weights)


def _ffn_call(layer, final, n_batch, n_seq, x2, consts, next_weights):
    tm = TM_FFN
    ns = n_seq // tm
    n_rows = n_batch * n_seq
    casts = [_cast_specs(w, layer + 1, n_batch * ns, ns) for w in next_weights]
    out_shape = (n_rows, D_MODEL) if final else (D_MODEL // LANES, n_rows, LANES)
    return pl.pallas_call(
        functools.partial(_ffn_kernel, tm, final, len(casts)),
        grid=(n_batch, ns),
        in_specs=[_lane_major_spec(tm, ns)] + [_const_spec(c.shape) for c in consts] + [c[0] for c in casts],
        out_specs=[_row_major_spec(tm, ns) if final else _lane_major_spec(tm, ns)] + [c[1] for c in casts],
        out_shape=[jax.ShapeDtypeStruct(out_shape, _F32)] + [c[2] for c in casts],
        scratch_shapes=[pltpu.VMEM((CARRY_ROWS, 2 * D_FF), _F32)]
        + ([pltpu.VMEM((D_MODEL // LANES, tm, LANES), _F32)] if final else []),
        compiler_params=pltpu.CompilerParams(
            dimension_semantics=("arbitrary", "arbitrary"),
            vmem_limit_bytes=VMEM_LIMIT_BYTES),
        name=f"ffn_l{layer}",
    )(x2, *consts, *next_weights)


def kernel(x, mix_norm, w_in, gm_ln_g, gm_ln_b, gm_ws, gm_bs, hg_lb_logits, hg_norm_g, w_br_gm, w_br_hg,
           w_out, ffn_norm, w_up, conv_w, conv_b, w_down, final_norm):
    n_batch, n_seq, d = x.shape
    depth = w_in.shape[0]
    assert d == D_MODEL and w_in.shape[2] == IN_COLS
    assert n_seq % TM_MIX == 0 and n_seq % TM_FFN == 0 and TM_MIX % GM_BLOCK == 0 and TM_MIX % HG_CHUNK == 0
    assert sum(FFN_COL_BLOCKS) == D_FF and all(w % MXU_DIM == 0 for w in FFN_COL_BLOCKS)
    assert TM_FFN % ROW_GROUP == 0
    assert (HG_CHUNK >> HG_WIDE_LEVELS) % SUBLANES == 0 and HG_LEVELS % 2 == 0
    pmat = jnp.asarray(_range_sum_matrix(HG_CHUNK, HG_WIDE_LEVELS), _BF16)
    masks = jnp.asarray(_level_masks(HG_CHUNK), _F32)
    x2 = x.reshape(n_batch * n_seq, d)
    row = lambda v: v.reshape(1, -1)
    mixer_stacks = (w_in, w_br_gm, w_br_hg, w_out)
    win_b, wgm_b, whg_b, wout_b = (w[0].astype(_BF16) for w in mixer_stacks)
    for l in range(depth):
        x2, wup_b, wdn_b = _mixer_call(
            l, n_batch, n_seq, x2,
            (row(mix_norm[l]), win_b, row(gm_ln_g[l]), row(gm_ln_b[l]), gm_ws[l], gm_bs[l].T, hg_lb_logits,
             row(hg_norm_g[l]), wgm_b, whg_b, wout_b, pmat, masks),
            (w_up, w_down))
        last = l == depth - 1
        x2, *next_b = _ffn_call(
            l, last, n_batch, n_seq, x2,
            (row(ffn_norm[l]), wup_b, conv_w[l], row(conv_b[l]), wdn_b, row(final_norm)),
            () if last else mixer_stacks)
        if not last:
            win_b, wgm_b, whg_b, wout_b = next_b
    return x2.reshape(n_batch, n_seq, d)
```

```python
import functools

import numpy as np
import jax
import jax.numpy as jnp
from jax import lax
from jax.experimental import pallas as pl
from jax.experimental.pallas import tpu as pltpu

D_MODEL = 1024
STREAM_CHUNK = 64
GM_BLOCK = 128
GM_HEADS = 4
GM_HEAD_DIM = 128
GM_DIM = GM_HEADS * GM_HEAD_DIM
HG_HEADS = 4
HG_DK = 128
HG_DV = 128
HG_DIM = HG_HEADS * HG_DK
D_FF = 2816
CONV_W = 3
EPS = 1e-6
TINY = 1e-30
LB_MAX = 0.999

COL_UV = 0
COL_HG = 2 * GM_DIM
COL_GATES = COL_HG + 4 * HG_DIM
IN_COLS = COL_GATES + 2 * D_MODEL

HG_CHUNK = 64
HG_LEVELS = 6
SUBLANES = 8
LANES = 128
BF16_TILE_ROWS = 16
HG_WIDE_LEVELS = 3
TM_MIX = 512
TM_FFN = 512
MXU_DIM = 256
FFN_COL_BLOCKS = (1536, 1280)
ROW_GROUP = SUBLANES * SUBLANES
CARRY_ROWS = 2 * SUBLANES
VMEM_LIMIT_BYTES = 56 * 1024 * 1024

_F32 = jnp.float32
_BF16 = jnp.bfloat16


def _range_sum_matrix(c, first_level):
    t = np.arange(c)
    mats = [t[None, :] <= t[:, None]]
    s = (c // 2) >> first_level
    while s >= 1:
        x = np.zeros((c, c), bool)
        for i in range(c):
            base = (i // (2 * s)) * 2 * s
            mid = base + s - 1
            if i - base >= s:
                x[i, mid + 1:i + 1] = True
            else:
                x[i, i + 1:mid + 1] = True
        mats.append(x)
        s //= 2
    p = np.concatenate(mats, 0).astype(np.float32)
    return np.concatenate([p, p, p], 1)


def _level_masks(c):
    i = np.arange(c)[:, None]
    j = np.arange(c)[None, :]
    out = []
    s = c // 2
    while s >= 1:
        out.append((i // (2 * s) == j // (2 * s)) & (i % (2 * s) >= s) & (j % (2 * s) < s))
        s //= 2
    out.append(i == j)
    wide = np.zeros((len(out), c, 2 * c), np.float32)
    for n, m in enumerate(out):
        wide[n, :, (n % 2) * c:(n % 2 + 1) * c] = m
    return wide


def _rms(x, gain):
    r = lax.rsqrt(jnp.mean(x * x, axis=-1, keepdims=True) + EPS)
    return (x * r) * gain


LOG2E = 1.4426950408889634


def _sigmoid(x):
    return 1.0 / (1.0 + jnp.exp2(x * (-LOG2E)))


def _gelu_exact(x):
    return 0.5 * x * (1.0 + lax.erf(x * np.float32(np.sqrt(0.5))))


def _dot(a, b):
    return jnp.dot(a, b, preferred_element_type=_F32)


def _dot_tn(a, b):
    return lax.dot_general(a, b, (((0,), (0,)), ((), ())), preferred_element_type=_F32)


def _cast_slabs(f32_refs, bf16_refs):
    for src, dst in zip(f32_refs, bf16_refs, strict=True):
        dst[...] = src[...].astype(_BF16)


def _load_rows(x_ref, lane_major):
    if not lane_major:
        return x_ref[...]
    return jnp.concatenate([x_ref[k] for k in range(D_MODEL // LANES)], axis=1)


def _mixer_kernel(layer, tm, x_lane_major,
                  x_ref, nrm_ref, win_ref, lng_ref, lnb_ref, ws_ref, bst_ref, lbl_ref, hgn_ref,
                  wgm_ref, whg_ref, wout_ref, p_ref, msk_ref, wup_f32_ref, wdn_f32_ref,
                  o_ref, wup_bf16_ref, wdn_bf16_ref,
                  st_ref, q_scr, k_scr, g_scr, v_scr, b_scr, gc_scr, e_scr, kt_scr, stt_scr, upd_scr,
                  qp_scr, w_scr, qb_scr, kb_scr):
    @pl.when(pl.program_id(1) == 0)
    def _():
        st_ref[...] = jnp.zeros_like(st_ref)

    _cast_slabs((wup_f32_ref, wdn_f32_ref), (wup_bf16_ref, wdn_bf16_ref))

    h = _rms(_load_rows(x_ref, x_lane_major), nrm_ref[...]).astype(_BF16)

    def in_proj(col0, width):
        return _dot(h, win_ref[:, col0:col0 + width])

    logits = lbl_ref[...]
    e = jnp.exp(logits - jnp.max(logits, axis=0, keepdims=True))
    p = e / jnp.sum(e, axis=0, keepdims=True)
    lb = jnp.zeros((1, HG_DIM), _F32)
    for j in range(1, layer + 1):
        lb = lb + p[j:j + 1, :]
    lb = jnp.clip(lb, 0.0, LB_MAX)
    q = in_proj(COL_HG, HG_DIM)
    f = in_proj(COL_HG + HG_DIM, HG_DIM)
    qt = q * _sigmoid(q)
    q_scr[...] = qt
    qb_scr[...] = qt.astype(_BF16)
    sig_f = _sigmoid(f)
    g_scr[...] = jnp.log(jnp.maximum(lb + (1.0 - lb) * sig_f, TINY)) * LOG2E
    kk = (1.0 - lb) * (1.0 - sig_f)
    k_scr[...] = kk
    kb_scr[...] = kk.astype(_BF16)
    v_scr[...] = in_proj(COL_HG + 2 * HG_DIM, HG_DIM).astype(_BF16)
    og = in_proj(COL_HG + 3 * HG_DIM, HG_DIM)
    ogs = og * _sigmoid(og)
    uv = in_proj(COL_UV, 2 * GM_DIM)

    def gmlp_branch():
        ug = _gelu_exact(uv[:, :GM_DIM])
        vg = _gelu_exact(uv[:, GM_DIM:])
        mu = jnp.mean(vg, axis=-1, keepdims=True)
        vc = vg - mu
        var = jnp.mean(vc * vc, axis=-1, keepdims=True)
        vln = ((vc * lax.rsqrt(var + EPS)) * lng_ref[...] + lnb_ref[...]).astype(_BF16)
        pi = lax.broadcasted_iota(jnp.int32, (GM_BLOCK, GM_BLOCK), 0) // STREAM_CHUNK
        pj = lax.broadcasted_iota(jnp.int32, (GM_BLOCK, GM_BLOCK), 1) // STREAM_CHUNK
        causal = pi >= pj
        wms = [jnp.where(causal, ws_ref[hh], 0.0).astype(_BF16) for hh in range(GM_HEADS)]
        blocks = []
        for n in range(tm // GM_BLOCK):
            heads = []
            for hh in range(GM_HEADS):
                vb = vln[n * GM_BLOCK:(n + 1) * GM_BLOCK, hh * GM_HEAD_DIM:(hh + 1) * GM_HEAD_DIM]
                heads.append(_dot(wms[hh], vb) + bst_ref[:, hh:hh + 1])
            blocks.append(jnp.concatenate(heads, axis=1))
        mixed = jnp.concatenate(blocks, axis=0)
        a = (ug * mixed).astype(_BF16)
        return _dot(a, wgm_ref[...])

    def gate_quarter(n):
        c0 = COL_GATES + n * (D_MODEL // 2)
        return _dot(h, win_ref[:, c0:c0 + D_MODEL // 2])

    c = HG_CHUNK
    for ci in range(tm // c):
        rows = slice(ci * c, (ci + 1) * c)
        g = g_scr[rows, :]
        g_hi = g.astype(_BF16)
        g_r1 = g - g_hi.astype(_F32)
        g_mid = g_r1.astype(_BF16)
        g_lo = (g_r1 - g_mid.astype(_F32)).astype(_BF16)
        sums = _dot(p_ref[...], jnp.concatenate([g_hi, g_mid, g_lo], axis=0))
        gcum = sums[0:c, :]
        gc_scr[rows, :] = gcum
        for lv in range(HG_WIDE_LEVELS):
            s = c >> (lv + 1)
            pieces = []
            for base in range(0, c, 2 * s):
                gmid = gcum[base + s - 1:base + s, :]
                pieces += [gmid - gcum[base:base + s, :], gcum[base + s:base + 2 * s, :] - gmid]
            e_scr[lv, rows, :] = jnp.exp2(jnp.concatenate(pieces, axis=0)).astype(_BF16)
        for n in range(HG_LEVELS - HG_WIDE_LEVELS):
            e_scr[HG_WIDE_LEVELS + n, rows, :] = jnp.exp2(sums[(1 + n) * c:(2 + n) * c, :]).astype(_BF16)

    gq0 = gate_quarter(0)

    for ci in range(tm // c):
        rows = slice(ci * c, (ci + 1) * c)
        for hh in range(HG_HEADS):
            hs = slice(hh * HG_DK, (hh + 1) * HG_DK)
            gcum = gc_scr[rows, hs]
            kd = (k_scr[rows, hs] * jnp.exp2(gcum[c - 1:c, :] - gcum)).astype(_BF16)
            upd_scr[ci, hh] = _dot_tn(v_scr[rows, hs], kd)
    for hh in range(HG_HEADS):
        hs = slice(hh * HG_DK, (hh + 1) * HG_DK)
        st = st_ref[hh]
        for ci in range(tm // c):
            stt_scr[ci, hh] = st.astype(_BF16).T
            st = st * jnp.exp2(gc_scr[(ci + 1) * c - 1:(ci + 1) * c, hs]) + upd_scr[ci, hh]
        st_ref[hh] = st

    gq1 = gate_quarter(1)

    for ci in range(tm // c):
        rows = slice(ci * c, (ci + 1) * c)
        for hh in range(HG_HEADS):
            hs = slice(hh * HG_DK, (hh + 1) * HG_DK)
            qh = qb_scr[rows, hs]
            kh = kb_scr[rows, hs]
            for lv in range(0, HG_LEVELS, 2):
                e0 = e_scr[lv, rows, hs]
                e1 = e_scr[lv + 1, rows, hs]
                qp_scr[ci, hh, lv // 2] = jnp.concatenate([qh * e0, qh * e1], axis=0)
                kt_scr[ci, hh, lv // 2] = jnp.concatenate([kh * e0, kh * e1], axis=0).T

    gq2 = gate_quarter(2)

    for ci in range(tm // c):
        rows = slice(ci * c, (ci + 1) * c)
        for hh in range(HG_HEADS):
            hs = slice(hh * HG_DK, (hh + 1) * HG_DK)
            w = msk_ref[HG_LEVELS] * jnp.sum(q_scr[rows, hs] * k_scr[rows, hs], axis=-1, keepdims=True)
            for lv in range(0, HG_LEVELS, 2):
                r = _dot(qp_scr[ci, hh, lv // 2], kt_scr[ci, hh, lv // 2])
                w = w + msk_ref[lv] * r[0:c, :] + msk_ref[lv + 1] * r[c:2 * c, :]
            w_scr[rows, hs] = w.astype(_BF16)

    gq3 = gate_quarter(3)

    for ci in range(tm // c):
        rows = slice(ci * c, (ci + 1) * c)
        for hh in range(HG_HEADS):
            hs = slice(hh * HG_DK, (hh + 1) * HG_DK)
            vh = v_scr[rows, hs]
            qe = (q_scr[rows, hs] * jnp.exp2(gc_scr[rows, hs])).astype(_BF16)
            o = _dot(w_scr[rows, hs], jnp.concatenate([vh, vh], axis=0)) + _dot(qe, stt_scr[ci, hh])
            b_scr[rows, hs] = _rms(o, hgn_ref[...])

    ya = gmlp_branch()
    b = (b_scr[...] * ogs).astype(_BF16)
    yb = _dot(b, whg_ref[...])

    y = _sigmoid(jnp.concatenate([gq0, gq1], axis=1)) * ya + _sigmoid(jnp.concatenate([gq2, gq3], axis=1)) * yb
    out = _load_rows(x_ref, x_lane_major) + _dot(y.astype(_BF16), wout_ref[...])
    for k in range(D_MODEL // LANES):
        o_ref[k] = out[:, k * LANES:(k + 1) * LANES]


def _ffn_kernel(tm, final, n_cast,
                x_ref, nrm_ref, wup_ref, cw_ref, cb_ref, wdn_ref, fin_ref, *rest):
    cast_in, o_ref, cast_out = rest[:n_cast], rest[n_cast], rest[n_cast + 1:2 * n_cast + 1]
    carry_ref, *final_scr = rest[2 * n_cast + 1:]

    @pl.when(pl.program_id(1) == 0)
    def _():
        carry_ref[...] = jnp.zeros_like(carry_ref)

    _cast_slabs(cast_in, cast_out)

    n_groups = tm // ROW_GROUP
    slabs = [(g * ROW_GROUP + a, g * ROW_GROUP + a * SUBLANES) for g in range(n_groups) for a in range(SUBLANES)]
    n_lane_tiles = D_MODEL // LANES
    xp = jnp.concatenate(
        [jnp.concatenate([x_ref[k, pl.ds(t0, SUBLANES, stride=SUBLANES), :] for k in range(n_lane_tiles)], axis=1)
         for t0, _ in slabs], axis=0)
    h = _rms(xp, nrm_ref[...]).astype(_BF16)

    def wrap_fix(reg, prev_reg):
        sub0 = lax.broadcasted_iota(jnp.int32, reg.shape, 0) == 0
        return jnp.where(sub0, pltpu.roll(prev_reg, 1, axis=0), pltpu.roll(reg, 1, axis=0))

    def conv(col0, w):
        cols = slice(col0, col0 + w)
        z = _dot(h, wup_ref[:, cols])
        prev = carry_ref[:, cols]
        carry_ref[:, cols] = z[tm - CARRY_ROWS:tm, :]
        z1, z2 = [], []
        for g in range(n_groups):
            zg = z[g * ROW_GROUP:(g + 1) * ROW_GROUP, :]
            pg = prev if g == 0 else z[g * ROW_GROUP - CARRY_ROWS:g * ROW_GROUP, :]
            fix1 = wrap_fix(zg[7 * SUBLANES:, :], pg[SUBLANES:, :])
            fix2 = wrap_fix(zg[6 * SUBLANES:7 * SUBLANES, :], pg[:SUBLANES, :])
            z1 += [fix1, zg[:7 * SUBLANES, :]]
            z2 += [fix2, fix1, zg[:6 * SUBLANES, :]]
        cw = cw_ref[:, cols]
        return (cb_ref[:, cols] + cw[0:1, :] * jnp.concatenate(z2, axis=0)
                + cw[1:2, :] * jnp.concatenate(z1, axis=0) + cw[2:3, :] * z)

    acc = None
    col0 = 0
    for w in FFN_COL_BLOCKS:
        gate = conv(col0, w)
        val = conv(D_FF + col0, w)
        act = (gate * _sigmoid(gate) * val).astype(_BF16)
        down = _dot(act, wdn_ref[col0:col0 + w, :])
        acc = down if acc is None else acc + down
        col0 += w
    acc = xp + acc
    if final:
        acc = _rms(acc, fin_ref[...])
    dst = final_scr[0] if final else o_ref
    for t0, r0 in slabs:
        for k in range(n_lane_tiles):
            dst[k, pl.ds(t0, SUBLANES, stride=SUBLANES), :] = acc[r0:r0 + SUBLANES, k * LANES:(k + 1) * LANES]
    if final:
        for k in range(n_lane_tiles):
            o_ref[:, k * LANES:(k + 1) * LANES] = dst[k]


def _const_spec(shape):
    nd = len(shape)
    return pl.BlockSpec(shape, lambda b, s: (0,) * nd)


def _row_major_spec(tm, ns):
    return pl.BlockSpec((tm, D_MODEL), lambda b, s: (b * ns + s, 0))


def _lane_major_spec(tm, ns):
    return pl.BlockSpec((D_MODEL // LANES, tm, LANES), lambda b, s: (0, b * ns + s, 0))


def _cast_specs(stack, layer, n_steps, ns):
    rows, cols = stack.shape[1:]
    group = 1
    while (rows * group) % (n_steps * BF16_TILE_ROWS):
        group *= 2
    slab = rows * group // n_steps
    in_spec = pl.BlockSpec((None, slab, cols), lambda b, s: (layer, (b * ns + s) // group, 0))
    out_spec = pl.BlockSpec((slab, cols), lambda b, s: ((b * ns + s) // group, 0))
    return in_spec, out_spec, jax.ShapeDtypeStruct((rows, cols), _BF16)


def _mixer_call(layer, n_batch, n_seq, x2, consts, next_weights):
    tm = TM_MIX
    ns = n_seq // tm
    x_lane_major = x2.ndim == 3
    in_spec = _lane_major_spec(tm, ns) if x_lane_major else _row_major_spec(tm, ns)
    casts = [_cast_specs(w, layer, n_batch * ns, ns) for w in next_weights]
    n_chunks = tm // HG_CHUNK
    return pl.pallas_call(
        functools.partial(_mixer_kernel, layer, tm, x_lane_major),
        grid=(n_batch, ns),
        in_specs=[in_spec] + [_const_spec(c.shape) for c in consts] + [c[0] for c in casts],
        out_specs=[_lane_major_spec(tm, ns)] + [c[1] for c in casts],
        out_shape=[jax.ShapeDtypeStruct((D_MODEL // LANES, n_batch * n_seq, LANES), _F32)] + [c[2] for c in casts],
        scratch_shapes=[
            pltpu.VMEM((HG_HEADS, HG_DV, HG_DK), _F32),
            pltpu.VMEM((tm, HG_DIM), _F32),
            pltpu.VMEM((tm, HG_DIM), _F32),
            pltpu.VMEM((tm, HG_DIM), _F32),
            pltpu.VMEM((tm, HG_DIM), _BF16),
            pltpu.VMEM((tm, HG_DIM), _F32),
            pltpu.VMEM((tm, HG_DIM), _F32),
            pltpu.VMEM((HG_LEVELS, tm, HG_DIM), _BF16),
            pltpu.VMEM((n_chunks, HG_HEADS, HG_LEVELS // 2, HG_DK, 2 * HG_CHUNK), _BF16),
            pltpu.VMEM((n_chunks, HG_HEADS, HG_DK, HG_DV), _BF16),
            pltpu.VMEM((n_chunks, HG_HEADS, HG_DV, HG_DK), _F32),
            pltpu.VMEM((n_chunks, HG_HEADS, HG_LEVELS // 2, 2 * HG_CHUNK, HG_DK), _BF16),
            pltpu.VMEM((tm, HG_DIM), _BF16),
            pltpu.VMEM((tm, HG_DIM), _BF16),
            pltpu.VMEM((tm, HG_DIM), _BF16),
        ],
        compiler_params=pltpu.CompilerParams(
            dimension_semantics=("arbitrary", "arbitrary"),
            vmem_limit_bytes=VMEM_LIMIT_BYTES),
        name=f"mixer_l{layer}",
    )(x2, *consts, *next_weights)


def _ffn_call(layer, final, n_batch, n_seq, x2, consts, next_weights):
    tm = TM_FFN
    ns = n_seq // tm
    n_rows = n_batch * n_seq
    casts = [_cast_specs(w, layer + 1, n_batch * ns, ns) for w in next_weights]
    out_shape = (n_rows, D_MODEL) if final else (D_MODEL // LANES, n_rows, LANES)
    return pl.pallas_call(
        functools.partial(_ffn_kernel, tm, final, len(casts)),
        grid=(n_batch, ns),
        in_specs=[_lane_major_spec(tm, ns)] + [_const_spec(c.shape) for c in consts] + [c[0] for c in casts],
        out_specs=[_row_major_spec(tm, ns) if final else _lane_major_spec(tm, ns)] + [c[1] for c in casts],
        out_shape=[jax.ShapeDtypeStruct(out_shape, _F32)] + [c[2] for c in casts],
        scratch_shapes=[pltpu.VMEM((CARRY_ROWS, 2 * D_FF), _F32)]
        + ([pltpu.VMEM((D_MODEL // LANES, tm, LANES), _F32)] if final else []),
        compiler_params=pltpu.CompilerParams(
            dimension_semantics=("arbitrary", "arbitrary"),
            vmem_limit_bytes=VMEM_LIMIT_BYTES),
        name=f"ffn_l{layer}",
    )(x2, *consts, *next_weights)


def kernel(x, mix_norm, w_in, gm_ln_g, gm_ln_b, gm_ws, gm_bs, hg_lb_logits, hg_norm_g, w_br_gm, w_br_hg,
           w_out, ffn_norm, w_up, conv_w, conv_b, w_down, final_norm):
    n_batch, n_seq, d = x.shape
    depth = w_in.shape[0]
    assert d == D_MODEL and w_in.shape[2] == IN_COLS
    assert n_seq % TM_MIX == 0 and n_seq % TM_FFN == 0 and TM_MIX % GM_BLOCK == 0 and TM_MIX % HG_CHUNK == 0
    assert sum(FFN_COL_BLOCKS) == D_FF and all(w % MXU_DIM == 0 for w in FFN_COL_BLOCKS)
    assert TM_FFN % ROW_GROUP == 0
    assert (HG_CHUNK >> HG_WIDE_LEVELS) % SUBLANES == 0 and HG_LEVELS % 2 == 0
    pmat = jnp.asarray(_range_sum_matrix(HG_CHUNK, HG_WIDE_LEVELS), _BF16)
    masks = jnp.asarray(_level_masks(HG_CHUNK), _F32)
    x2 = x.reshape(n_batch * n_seq, d)
    row = lambda v: v.reshape(1, -1)
    mixer_stacks = (w_in, w_br_gm, w_br_hg, w_out)
    win_b, wgm_b, whg_b, wout_b = (w[0].astype(_BF16) for w in mixer_stacks)
    for l in range(depth):
        x2, wup_b, wdn_b = _mixer_call(
            l, n_batch, n_seq, x2,
            (row(mix_norm[l]), win_b, row(gm_ln_g[l]), row(gm_ln_b[l]), gm_ws[l], gm_bs[l].T, hg_lb_logits,
             row(hg_norm_g[l]), wgm_b, whg_b, wout_b, pmat, masks),
            (w_up, w_down))
        last = l == depth - 1
        x2, *next_b = _ffn_call(
            l, last, n_batch, n_seq, x2,
            (row(ffn_norm[l]), wup_b, conv_w[l], row(conv_b[l]), wdn_b, row(final_norm)),
            () if last else mixer_stacks)
        if not last:
            win_b, wgm_b, whg_b, wout_b = next_b
    return x2.reshape(n_batch, n_seq, d)
```
